```python
import math
import jax, jax.numpy as jnp
from jax import lax
import numpy as np

D_MODEL = 1024
BATCH = 16
SEQ = 2048
DEPTH = 4
DEC_BATCH = 8
DEC_SEQ = 16
PAST_LEN = 2048

CHUNK = 64
N_MIXERS = 4
Q_BLOCK = 128
EPS = 1e-6

GM_CHUNK = 128
D_GM = D_MODEL
GM_GROUPS = 8
GM_GW = D_GM // GM_GROUPS
DIFF_HEADS = 8
DIFF_HD = D_MODEL // (2 * DIFF_HEADS)
ROPE_THETA = 500000.0
ROT_DIM = DIFF_HD // 4
D_SC = D_MODEL
SC_W = 3
SB_HEADS = 16
SB_HD = D_MODEL // SB_HEADS
D_FF = 2816
FFN_W = 3

N_A = (DEPTH + 3) // N_MIXERS
N_B = (DEPTH + 2) // N_MIXERS
N_C = (DEPTH + 1) // N_MIXERS
N_D = DEPTH // N_MIXERS

kernel_name = 'hybrid_streaming_encoder_step'


def rms_norm(x, g):
    xf = x.astype(jnp.float32)
    y = xf * lax.rsqrt(jnp.mean(xf * xf, axis=-1, keepdims=True) + EPS)
    return (y * g.astype(jnp.float32)).astype(x.dtype)


def layer_norm(x, g, b):
    xf = x.astype(jnp.float32)
    mu = jnp.mean(xf, axis=-1, keepdims=True)
    xc = xf - mu
    y = xc * lax.rsqrt(jnp.mean(xc * xc, axis=-1, keepdims=True) + EPS)
    return (y * g.astype(jnp.float32) + b.astype(jnp.float32)).astype(x.dtype)


def causal_dwconv(x, prev, w):
    width, S = w.shape[0], x.shape[1]
    xp = jnp.concatenate([prev.astype(x.dtype), x], axis=1)
    y = w[width - 1] * xp[:, width - 1:width - 1 + S]
    for k in range(width - 1):
        y = y + w[k] * xp[:, k:k + S]
    return y, xp[:, S:]


def partial_rope(x, pos):
    half = ROT_DIM // 2
    inv = ROPE_THETA ** (-jnp.arange(0, ROT_DIM, 2, dtype=jnp.float32) / ROT_DIM)
    ang = pos.astype(jnp.float32)[:, None] * inv[None, :]
    ang = ang.reshape((ang.shape[0],) + (1,) * (x.ndim - 3) + (half,))
    cos, sin = jnp.cos(ang), jnp.sin(ang)
    xf = x.astype(jnp.float32)
    x1, x2, rest = xf[..., :half], xf[..., half:ROT_DIM], xf[..., ROT_DIM:]
    out = jnp.concatenate([x1 * cos - x2 * sin, x2 * cos + x1 * sin, rest], axis=-1)
    return out.astype(x.dtype)


def sweep_blocks(fn, q, pos):
    B, S = q.shape[0], q.shape[1]
    nb = S // Q_BLOCK
    qb = jnp.moveaxis(q.reshape((B, nb, Q_BLOCK) + q.shape[2:]), 1, 0)
    pb = pos.reshape(nb, Q_BLOCK)
    ob = lax.map(lambda a: fn(a[0], a[1]), (qb, pb))
    return jnp.moveaxis(ob, 0, 1).reshape((B, S) + ob.shape[3:])


def diff_attn_block(q, k, v, q_pos, lam):
    K = k.shape[1]
    s = jnp.einsum('bqhcd,bkhcd->bhcqk', q.astype(jnp.float32), k.astype(jnp.float32)) * (DIFF_HD ** -0.5)
    mask = (jnp.arange(K)[None, :] // CHUNK) <= (q_pos[:, None] // CHUNK)
    p = jax.nn.softmax(jnp.where(mask, s, -jnp.inf), axis=-1)
    a = p[:, :, 0] - lam * p[:, :, 1]
    return jnp.einsum('bhqk,bkhe->bqhe', a, v.astype(jnp.float32))


def stick_breaking_block(q, k, v, q_pos):
    K = k.shape[1]
    z = jnp.einsum('bqhd,bkhd->bhqk', q.astype(jnp.float32), k.astype(jnp.float32)) * (SB_HD ** -0.5)
    mask = jnp.arange(K)[None, :] < q_pos[:, None]
    log_beta = jax.nn.log_sigmoid(z)
    log_keep = jnp.where(mask, jax.nn.log_sigmoid(-z), 0.0)
    after = lax.cumsum(log_keep, axis=3, reverse=True) - log_keep
    w = jnp.where(mask, jnp.exp(log_beta + after), 0.0)
    return jnp.einsum('bhqk,bkhd->bqhd', w, v.astype(jnp.float32))


def gmlp_mix(h, w_in, ln_g, ln_b, ws, bs, w_out):
    B, S, _ = h.shape
    L = GM_CHUNK if S >= GM_CHUNK else S
    u, v = jnp.split(jax.nn.gelu(h @ w_in), 2, axis=-1)
    v = layer_norm(v, ln_g, ln_b)
    vb = v.reshape(B, S // L, L, GM_GROUPS, GM_GW)
    w_s = jnp.tril(ws[:, :L, :L])
    mixed = jnp.einsum('gts,bnsgc->bntgc', w_s, vb) + bs[:, :L].T[None, None, :, :, None]
    return (u * mixed.reshape(B, S, D_GM)) @ w_out, v


def diff_mix(h, k_cache, v_cache, pos, w_qkv, lam_p, subln_g, w_out, lam_init):
    B, S, _ = h.shape
    q, k, v = jnp.split(h @ w_qkv, 3, axis=-1)
    q = partial_rope(q.reshape(B, S, DIFF_HEADS, 2, DIFF_HD), pos)
    k = partial_rope(k.reshape(B, S, DIFF_HEADS, 2, DIFF_HD), pos)
    v = v.reshape(B, S, DIFF_HEADS, 2 * DIFF_HD)
    lp = lam_p.astype(jnp.float32)
    lam = jnp.exp(jnp.sum(lp[0] * lp[1])) - jnp.exp(jnp.sum(lp[2] * lp[3])) + lam_init
    if k_cache is None:
        keys, vals = k, v
        o = sweep_blocks(lambda qb, pb: diff_attn_block(qb, keys, vals, pb, lam), q, pos)
    else:
        keys = jnp.concatenate([k_cache.astype(k.dtype), k], axis=1)
        vals = jnp.concatenate([v_cache.astype(v.dtype), v], axis=1)
        o = diff_attn_block(q, keys, vals, pos, lam)
    o = rms_norm(o, subln_g) * (1.0 - lam_init)
    return o.astype(h.dtype).reshape(B, S, D_MODEL) @ w_out, k, v


def short_conv_mix(h, prev, w_in, conv_w, w_out):
    b_gate, c_gate, xin = jnp.split(h @ w_in, 3, axis=-1)
    y, new_prev = causal_dwconv(c_gate * xin, prev, conv_w)
    return (b_gate * y) @ w_out, new_prev


def sb_mix(h, k_cache, v_cache, pos, w_qkv, w_out):
    B, S, _ = h.shape
    q, k, v = [t.reshape(B, S, SB_HEADS, SB_HD) for t in jnp.split(h @ w_qkv, 3, axis=-1)]
    if k_cache is None:
        keys, vals = k, v
        o = sweep_blocks(lambda qb, pb: stick_breaking_block(qb, keys, vals, pb), q, pos)
    else:
        keys = jnp.concatenate([k_cache.astype(k.dtype), k], axis=1)
        vals = jnp.concatenate([v_cache.astype(v.dtype), v], axis=1)
        o = stick_breaking_block(q, keys, vals, pos)
    return o.astype(h.dtype).reshape(B, S, D_MODEL) @ w_out, k, v


def conv_ffn(h, prev, w_in, conv_w, conv_b, w_out):
    g, u = jnp.split(h @ w_in, 2, axis=-1)
    g, new_prev = causal_dwconv(g, prev, conv_w)
    return (jax.nn.silu(g + conv_b) * u) @ w_out, new_prev


def trunk(x, c, pos, diff_k_c, diff_v_c, sconv_c, sb_k_c, sb_v_c, ffn_c,
          ada_w, ada_b, norm_g, gm_w_in, gm_ln_g, gm_ln_b, gm_ws, gm_bs, gm_w_out,
          diff_w_qkv, diff_lambda, diff_subln_g, diff_w_out,
          sc_w_in, sc_conv_w, sc_w_out, sb_w_qkv, sb_w_out,
          ffn_w_in, ffn_conv_w, ffn_conv_b, ffn_w_out):
    streaming = diff_k_c is not None
    Bsz = x.shape[0]
    cs = jax.nn.silu(c)
    gm_v, dk, dv, scs, sbk, sbv, ffs = [], [], [], [], [], [], []
    for i in range(DEPTH):
        kind, j = i % N_MIXERS, i // N_MIXERS
        mod = (cs @ ada_w[i] + ada_b[i]).reshape(Bsz, 6, 1, D_MODEL)
        h = rms_norm(x, norm_g[i, 0]) * (1 + mod[:, 1]) + mod[:, 0]
        if kind == 0:
            m, v_rows = gmlp_mix(h, gm_w_in[j], gm_ln_g[j], gm_ln_b[j], gm_ws[j], gm_bs[j], gm_w_out[j])
            gm_v.append(v_rows)
        elif kind == 1:
            m, k_new, v_new = diff_mix(h, diff_k_c[j] if streaming else None,
                                       diff_v_c[j] if streaming else None, pos,
                                       diff_w_qkv[j], diff_lambda[j], diff_subln_g[j], diff_w_out[j],
                                       0.8 - 0.6 * math.exp(-0.3 * i))
            dk.append(k_new)
            dv.append(v_new)
        elif kind == 2:
            prev = sconv_c[j] if streaming else jnp.zeros((Bsz, SC_W - 1, D_SC), x.dtype)
            m, st = short_conv_mix(h, prev, sc_w_in[j], sc_conv_w[j], sc_w_out[j])
            scs.append(st)
        else:
            m, k_new, v_new = sb_mix(h, sb_k_c[j] if streaming else None,
                                     sb_v_c[j] if streaming else None, pos, sb_w_qkv[j], sb_w_out[j])
            sbk.append(k_new)
            sbv.append(v_new)
        x = x + mod[:, 2] * rms_norm(m, norm_g[i, 1])
        h = rms_norm(x, norm_g[i, 2]) * (1 + mod[:, 4]) + mod[:, 3]
        prev = ffn_c[i] if streaming else jnp.zeros((Bsz, FFN_W - 1, D_FF), x.dtype)
        f, st = conv_ffn(h, prev, ffn_w_in[i], ffn_conv_w[i], ffn_conv_b[i], ffn_w_out[i])
        ffs.append(st)
        x = x + mod[:, 5] * rms_norm(f, norm_g[i, 3])
    return (x, jnp.stack(gm_v), jnp.stack(dk), jnp.stack(dv), jnp.stack(scs),
            jnp.stack(sbk), jnp.stack(sbv), jnp.stack(ffs))


def setup_inputs(seed: int = 0) -> dict:
    key = jax.random.key(seed)
    ks = iter(jax.random.split(key, 40))
    D = D_MODEL

    def nrm(shape, scale):
        return scale * jax.random.normal(next(ks), shape, jnp.float32)

    return {
        'x_prompt': nrm((BATCH, SEQ, D), 1.0),
        'x_sample': nrm((DEC_BATCH, DEC_SEQ, D), 1.0),
        'cache_diff_k': nrm((N_B, DEC_BATCH, PAST_LEN, DIFF_HEADS, 2, DIFF_HD), 1.0),
        'cache_diff_v': nrm((N_B, DEC_BATCH, PAST_LEN, DIFF_HEADS, 2 * DIFF_HD), 1.0),
        'state_sconv': nrm((N_C, DEC_BATCH, SC_W - 1, D_SC), 1.0),
        'cache_sb_k': nrm((N_D, DEC_BATCH, PAST_LEN, SB_HEADS, SB_HD), 1.0),
        'cache_sb_v': nrm((N_D, DEC_BATCH, PAST_LEN, SB_HEADS, SB_HD), 1.0),
        'state_ffn_conv': nrm((DEPTH, DEC_BATCH, FFN_W - 1, D_FF), 1.0),
        'c_prompt': nrm((BATCH, D), 1.0),
        'c_sample': nrm((DEC_BATCH, D), 1.0),
        'ada_w': nrm((DEPTH, D, 6 * D), 0.5 * D ** -0.5),
        'ada_b': nrm((DEPTH, 6 * D), 0.02),
        'norm_g': 1.0 + nrm((DEPTH, 4, D), 0.05),
        'gm_w_in': nrm((N_A, D, 2 * D_GM), D ** -0.5),
        'gm_ln_g': 1.0 + nrm((N_A, D_GM), 0.05),
        'gm_ln_b': nrm((N_A, D_GM), 0.02),
        'gm_ws': nrm((N_A, GM_GROUPS, GM_CHUNK, GM_CHUNK), GM_CHUNK ** -0.5),
        'gm_bs': nrm((N_A, GM_GROUPS, GM_CHUNK), 0.02),
        'gm_w_out': nrm((N_A, D_GM, D), D_GM ** -0.5),
        'diff_w_qkv': nrm((N_B, D, 3 * D), D ** -0.5),
        'diff_lambda': nrm((N_B, 4, DIFF_HD), 0.1),
        'diff_subln_g': 1.0 + nrm((N_B, 2 * DIFF_HD), 0.05),
        'diff_w_out': nrm((N_B, D, D), D ** -0.5),
        'sc_w_in': nrm((N_C, D, 3 * D_SC), D ** -0.5),
        'sc_conv_w': nrm((N_C, SC_W, D_SC), SC_W ** -0.5),
        'sc_w_out': nrm((N_C, D_SC, D), D_SC ** -0.5),
        'sb_w_qkv': nrm((N_D, D, 3 * D), D ** -0.5),
        'sb_w_out': nrm((N_D, D, D), D ** -0.5),
        'ffn_w_in': nrm((DEPTH, D, 2 * D_FF), D ** -0.5),
        'ffn_conv_w': nrm((DEPTH, FFN_W, D_FF), FFN_W ** -0.5),
        'ffn_conv_b': nrm((DEPTH, D_FF), 0.02),
        'ffn_w_out': nrm((DEPTH, D_FF, D), D_FF ** -0.5),
    }


def reference(x_prompt, x_sample, cache_diff_k, cache_diff_v, state_sconv, cache_sb_k, cache_sb_v,
              state_ffn_conv, c_prompt, c_sample, ada_w, ada_b, norm_g,
              gm_w_in, gm_ln_g, gm_ln_b, gm_ws, gm_bs, gm_w_out,
              diff_w_qkv, diff_lambda, diff_subln_g, diff_w_out,
              sc_w_in, sc_conv_w, sc_w_out, sb_w_qkv, sb_w_out,
              ffn_w_in, ffn_conv_w, ffn_conv_b, ffn_w_out):
    weights = (ada_w, ada_b, norm_g, gm_w_in, gm_ln_g, gm_ln_b, gm_ws, gm_bs, gm_w_out,
               diff_w_qkv, diff_lambda, diff_subln_g, diff_w_out,
               sc_w_in, sc_conv_w, sc_w_out, sb_w_qkv, sb_w_out,
               ffn_w_in, ffn_conv_w, ffn_conv_b, ffn_w_out)
    pos_p = jnp.arange(x_prompt.shape[1])
    pos_s = cache_diff_k.shape[2] + jnp.arange(x_sample.shape[1])
    (y_prompt, _gm_v_prompt, diff_k_prompt, diff_v_prompt, sconv_prompt,
     sb_k_prompt, sb_v_prompt, ffn_conv_prompt) = trunk(
        x_prompt, c_prompt, pos_p, None, None, None, None, None, None, *weights)
    (y_sample, gm_v_sample, diff_k_sample, diff_v_sample, sconv_sample,
     sb_k_sample, sb_v_sample, ffn_conv_sample) = trunk(
        x_sample, c_sample, pos_s, cache_diff_k, cache_diff_v, state_sconv,
        cache_sb_k, cache_sb_v, state_ffn_conv, *weights)
    return (y_prompt, y_sample, gm_v_sample,
            diff_k_prompt, diff_v_prompt, diff_k_sample, diff_v_sample,
            sconv_prompt, sconv_sample,
            sb_k_prompt, sb_v_prompt, sb_k_sample, sb_v_sample,
            ffn_conv_prompt, ffn_conv_sample)
```

```python
import functools
import math

import jax
import jax.numpy as jnp
from jax import lax
from jax.experimental import pallas as pl
from jax.experimental.pallas import tpu as pltpu

F32 = jnp.float32
BF16 = jnp.bfloat16

EPS = 1e-6
CHUNK = 64
N_MIXERS = 4
GM_CHUNK = 128
GM_GROUPS = 8
DIFF_HEADS = 8
ROPE_THETA = 500000.0
SB_HEADS = 16

LANES = 128
CARRY_ROWS = 8
NEG_BIG = -1e30

ROW_TILE = 512
ATTN_TILE = 256
CACHE_TILE = 512
CONV_COLS = 256
VMEM_LIMIT = 56 * 1024 * 1024


def _params(n_axes):
    return pltpu.CompilerParams(dimension_semantics=("arbitrary",) * n_axes,
                                vmem_limit_bytes=VMEM_LIMIT)


def _const_spec(shape):
    zeros = (0,) * len(shape)
    return pl.BlockSpec(shape, lambda *_: zeros, pipeline_mode=pl.Buffered(1))


def _rms(x, g):
    return x * lax.rsqrt(jnp.mean(x * x, axis=-1, keepdims=True) + EPS) * g


def _prenorm(x, g, scale, shift):
    return _rms(x, g) * (1.0 + scale) + shift


def _dot(a, b):
    return jnp.dot(a, b, preferred_element_type=F32)


def _dot_nt(a, b):
    return lax.dot_general(a, b, (((1,), (1,)), ((), ())), preferred_element_type=F32)


def _mod_kernel(c_ref, w_ref, b_ref, o_ref):
    c = c_ref[...]
    cs = (c * jax.nn.sigmoid(c)).astype(BF16)
    o_ref[0] = _dot(cs, w_ref[0].astype(BF16)) + b_ref[0]


def _modulation(c_all, ada_w, ada_b):
    depth, d, _ = ada_w.shape
    rows = c_all.shape[0]
    out = pl.pallas_call(
        _mod_kernel,
        grid=(depth, 6),
        in_specs=[
            pl.BlockSpec((rows, d), lambda i, j: (0, 0)),
            pl.BlockSpec((1, d, d), lambda i, j: (i, 0, j)),
            pl.BlockSpec((1, 1, d), lambda i, j: (i * 6 + j, 0, 0)),
        ],
        out_specs=pl.BlockSpec((1, rows, d), lambda i, j: (i * 6 + j, 0, 0)),
        out_shape=jax.ShapeDtypeStruct((depth * 6, rows, d), F32),
        compiler_params=_params(2),
    )(c_all, ada_w, ada_b.reshape(depth * 6, 1, d))
    return out.reshape(depth, 6, rows, d)


def _conv3(val, cw, carry, buf, lo, width, tm):
    buf[0:CARRY_ROWS, :] = carry[:, lo:lo + width]
    buf[CARRY_ROWS:CARRY_ROWS + tm, :] = val
    back1 = buf[CARRY_ROWS - 1:CARRY_ROWS - 1 + tm, :]
    back2 = buf[CARRY_ROWS - 2:CARRY_ROWS - 2 + tm, :]
    carry[:, lo:lo + width] = buf[tm:tm + CARRY_ROWS, :]
    return cw[2:3] * val + cw[1:2] * back1 + cw[0:1] * back2


def _init_carry(carry, prev_ref):
    @pl.when(pl.program_id(1) == 0)
    def _():
        carry[...] = jnp.zeros_like(carry)
        carry[CARRY_ROWS - 2:CARRY_ROWS, :] = prev_ref[0]


def _ffn_kernel(x_ref, mod_ref, gpre_ref, gpost_ref, prev_ref, win_ref, cw_ref, cb_ref, wout_ref,
                y_ref, st_ref, carry, buf, *, tm, dff, fc):
    _init_carry(carry, prev_ref)
    x = x_ref[0]
    mod = mod_ref[0]
    h = _prenorm(x, gpre_ref[...], mod[4:5], mod[3:4]).astype(BF16)
    acc = jnp.zeros(x.shape, F32)
    for c in range(dff // fc):
        lo = c * fc
        g = _dot(h, win_ref[:, lo:lo + fc])
        u = _dot(h, win_ref[:, dff + lo:dff + lo + fc])
        y = _conv3(g, cw_ref[:, lo:lo + fc], carry, buf, lo, fc, tm) + cb_ref[:, lo:lo + fc]
        a = (y * jax.nn.sigmoid(y) * u).astype(BF16)
        acc = acc + _dot(a, wout_ref[lo:lo + fc, :])
    st_ref[0] = carry[CARRY_ROWS - 2:CARRY_ROWS, :]
    y_ref[0] = x + mod[5:6] * _rms(acc, gpost_ref[...])


def _ffn(x, mod, gpre, gpost, prev, w_in, conv_w, conv_b, w_out):
    b, s, d = x.shape
    dff = w_out.shape[0]
    tm = min(ROW_TILE, s)
    fc = CONV_COLS
    kern = functools.partial(_ffn_kernel, tm=tm, dff=dff, fc=fc)
    return pl.pallas_call(
        kern,
        grid=(b, s // tm),
        in_specs=[
            pl.BlockSpec((1, tm, d), lambda i, j: (i, j, 0)),
            pl.BlockSpec((1, 6, d), lambda i, j: (i, 0, 0)),
            _const_spec((1, d)),
            _const_spec((1, d)),
            pl.BlockSpec((1, 2, dff), lambda i, j: (i, 0, 0)),
            _const_spec((d, 2 * dff)),
            _const_spec((3, dff)),
            _const_spec((1, dff)),
            _const_spec((dff, d)),
        ],
        out_specs=[
            pl.BlockSpec((1, tm, d), lambda i, j: (i, j, 0)),
            pl.BlockSpec((1, 2, dff), lambda i, j: (i, 0, 0)),
        ],
        out_shape=[
            jax.ShapeDtypeStruct((b, s, d), F32),
            jax.ShapeDtypeStruct((b, 2, dff), F32),
        ],
        scratch_shapes=[
            pltpu.VMEM((CARRY_ROWS, dff), F32),
            pltpu.VMEM((CARRY_ROWS + tm, fc), F32),
        ],
        compiler_params=_params(2),
    )(x, mod, gpre, gpost, prev, w_in, conv_w, conv_b.reshape(1, dff), w_out)


def _sconv_kernel(x_ref, mod_ref, gpre_ref, gpost_ref, prev_ref, win_ref, cw_ref, wout_ref,
                  y_ref, st_ref, carry, buf, *, tm, dsc, fc):
    _init_carry(carry, prev_ref)
    x = x_ref[0]
    mod = mod_ref[0]
    h = _prenorm(x, gpre_ref[...], mod[1:2], mod[0:1]).astype(BF16)
    acc = jnp.zeros(x.shape, F32)
    for c in range(dsc // fc):
        lo = c * fc
        b_gate = _dot(h, win_ref[:, lo:lo + fc])
        c_gate = _dot(h, win_ref[:, dsc + lo:dsc + lo + fc])
        xin = _dot(h, win_ref[:, 2 * dsc + lo:2 * dsc + lo + fc])
        y = _conv3(c_gate * xin, cw_ref[:, lo:lo + fc], carry, buf, lo, fc, tm)
        acc = acc + _dot((b_gate * y).astype(BF16), wout_ref[lo:lo + fc, :])
    st_ref[0] = carry[CARRY_ROWS - 2:CARRY_ROWS, :]
    y_ref[0] = x + mod[2:3] * _rms(acc, gpost_ref[...])


def _sconv(x, mod, gpre, gpost, prev, w_in, conv_w, w_out):
    b, s, d = x.shape
    dsc = w_out.shape[0]
    tm = min(ROW_TILE, s)
    fc = CONV_COLS
    kern = functools.partial(_sconv_kernel, tm=tm, dsc=dsc, fc=fc)
    return pl.pallas_call(
        kern,
        grid=(b, s // tm),
        in_specs=[
            pl.BlockSpec((1, tm, d), lambda i, j: (i, j, 0)),
            pl.BlockSpec((1, 6, d), lambda i, j: (i, 0, 0)),
            _const_spec((1, d)),
            _const_spec((1, d)),
            pl.BlockSpec((1, 2, dsc), lambda i, j: (i, 0, 0)),
            _const_spec((d, 3 * dsc)),
            _const_spec((3, dsc)),
            _const_spec((dsc, d)),
        ],
        out_specs=[
            pl.BlockSpec((1, tm, d), lambda i, j: (i, j, 0)),
            pl.BlockSpec((1, 2, dsc), lambda i, j: (i, 0, 0)),
        ],
        out_shape=[
            jax.ShapeDtypeStruct((b, s, d), F32),
            jax.ShapeDtypeStruct((b, 2, dsc), F32),
        ],
        scratch_shapes=[
            pltpu.VMEM((CARRY_ROWS, dsc), F32),
            pltpu.VMEM((CARRY_ROWS + tm, fc), F32),
        ],
        compiler_params=_params(2),
    )(x, mod, gpre, gpost, prev, w_in, conv_w, w_out)


def _gmlp_kernel(x_ref, mod_ref, gpre_ref, gpost_ref, win_ref, lng_ref, lnb_ref, ws_ref, bsf_ref,
                 wout_ref, y_ref, v_ref, mixbuf, *, tm, span, dgm):
    x = x_ref[0]
    mod = mod_ref[0]
    h = _prenorm(x, gpre_ref[...], mod[1:2], mod[0:1]).astype(BF16)
    u = jax.nn.gelu(_dot(h, win_ref[:, 0:dgm]))
    v = jax.nn.gelu(_dot(h, win_ref[:, dgm:2 * dgm]))
    vc = v - jnp.mean(v, axis=-1, keepdims=True)
    v = vc * lax.rsqrt(jnp.mean(vc * vc, axis=-1, keepdims=True) + EPS) * lng_ref[...] + lnb_ref[...]
    v_ref[0] = v
    mix_dtype = BF16 if span % 16 == 0 and span >= LANES else F32
    vm = v.astype(mix_dtype)
    gw = dgm // GM_GROUPS
    row = lax.broadcasted_iota(jnp.int32, (span, span), 0)
    col = lax.broadcasted_iota(jnp.int32, (span, span), 1)
    for g in range(GM_GROUPS):
        w = jnp.where(row >= col, ws_ref[g], 0.0).astype(mix_dtype)
        for r in range(tm // span):
            mixbuf[r * span:(r + 1) * span, g * gw:(g + 1) * gw] = (
                _dot(w, vm[r * span:(r + 1) * span, g * gw:(g + 1) * gw]) + bsf_ref[:, g * gw:(g + 1) * gw])
    m = _dot((u * mixbuf[...]).astype(BF16), wout_ref[...])
    y_ref[0] = x + mod[2:3] * _rms(m, gpost_ref[...])


def _gmlp(x, mod, gpre, gpost, w_in, ln_g, ln_b, ws, bs, w_out):
    b, s, d = x.shape
    dgm = w_out.shape[0]
    span = GM_CHUNK if s >= GM_CHUNK else s
    tm = min(ROW_TILE, s)
    ws_l = ws[:, :span, :span]
    bs_tile = jnp.repeat(bs[:, :span].T, dgm // GM_GROUPS, axis=1)
    kern = functools.partial(_gmlp_kernel, tm=tm, span=span, dgm=dgm)
    return pl.pallas_call(
        kern,
        grid=(b, s // tm),
        in_specs=[
            pl.BlockSpec((1, tm, d), lambda i, j: (i, j, 0)),
            pl.BlockSpec((1, 6, d), lambda i, j: (i, 0, 0)),
            _const_spec((1, d)),
            _const_spec((1, d)),
            _const_spec((d, 2 * dgm)),
            _const_spec((1, dgm)),
            _const_spec((1, dgm)),
            _const_spec((GM_GROUPS, span, span)),
            _const_spec((span, dgm)),
            _const_spec((dgm, d)),
        ],
        out_specs=[
            pl.BlockSpec((1, tm, d), lambda i, j: (i, j, 0)),
            pl.BlockSpec((1, tm, dgm), lambda i, j: (i, j, 0)),
        ],
        out_shape=[
            jax.ShapeDtypeStruct((b, s, d), F32),
            jax.ShapeDtypeStruct((b, s, dgm), F32),
        ],
        scratch_shapes=[pltpu.VMEM((tm, dgm), F32)],
        compiler_params=_params(2),
    )(x, mod, gpre, gpost, w_in, ln_g.reshape(1, dgm), ln_b.reshape(1, dgm), ws_l, bs_tile, w_out)


def _rotate(t, cos, sin_fwd, sin_bwd, rot_half):
    outs = []
    for s in range(t.shape[1] // LANES):
        ts = t[:, s * LANES:(s + 1) * LANES]
        ahead = pltpu.roll(ts, LANES - rot_half, 1)
        behind = pltpu.roll(ts, rot_half, 1)
        outs.append(ts * cos + ahead * sin_fwd + behind * sin_bwd)
    return jnp.concatenate(outs, axis=1)


def _qkv_kernel(*refs, d, qscale, rot_half):
    if rot_half:
        (x_ref, mod_ref, gpre_ref, w_ref, cos_ref, sf_ref, sb_ref,
         k_ref, v_ref, qb_ref, kb_ref, vb_ref) = refs
    else:
        x_ref, mod_ref, gpre_ref, w_ref, k_ref, v_ref, qb_ref, kb_ref, vb_ref = refs
    x = x_ref[0]
    mod = mod_ref[0]
    h = _prenorm(x, gpre_ref[...], mod[1:2], mod[0:1]).astype(BF16)
    q = _dot(h, w_ref[:, 0:d])
    k = _dot(h, w_ref[:, d:2 * d])
    v = _dot(h, w_ref[:, 2 * d:3 * d])
    if rot_half:
        cos, sf, sb = cos_ref[...], sf_ref[...], sb_ref[...]
        q = _rotate(q, cos, sf, sb, rot_half)
        k = _rotate(k, cos, sf, sb, rot_half)
    k_ref[0] = k
    v_ref[0] = v
    qb_ref[0] = (q * qscale).astype(BF16)
    kb_ref[0] = k.astype(BF16)
    vb_ref[0] = v.astype(BF16)


def _rope_tables(pos, head_dim, rot_dim):
    half = rot_dim // 2
    inv = ROPE_THETA ** (-jnp.arange(0, rot_dim, 2, dtype=F32) / rot_dim)
    ang = pos.astype(F32)[:, None] * inv[None, :]
    cos, sin = jnp.cos(ang), jnp.sin(ang)
    n = pos.shape[0]
    pad = jnp.zeros((n, head_dim - rot_dim), F32)
    zeros = jnp.zeros((n, half), F32)
    cos_t = jnp.concatenate([cos, cos, pad + 1.0], axis=1)
    sf_t = jnp.concatenate([-sin, zeros, pad], axis=1)
    sb_t = jnp.concatenate([zeros, sin, pad], axis=1)
    rep = LANES // head_dim
    return tuple(jnp.tile(t, (1, rep)) for t in (cos_t, sf_t, sb_t))


def _qkv(x, mod, gpre, w_qkv, head_dim, rope=None):
    b, s, d = x.shape
    tm = min(ROW_TILE, s)
    rot_half = 0
    tables = ()
    table_specs = []
    if rope is not None:
        pos, rot_dim = rope
        rot_half = rot_dim // 2
        tables = _rope_tables(pos, head_dim, rot_dim)
        table_specs = [pl.BlockSpec((tm, LANES), lambda i, j: (j, 0))] * 3
    kern = functools.partial(_qkv_kernel, d=d, qscale=head_dim ** -0.5, rot_half=rot_half)
    tile = pl.BlockSpec((1, tm, d), lambda i, j: (i, j, 0))
    return pl.pallas_call(
        kern,
        grid=(b, s // tm),
        in_specs=[tile, pl.BlockSpec((1, 6, d), lambda i, j: (i, 0, 0)), _const_spec((1, d)),
                  _const_spec((d, 3 * d))] + table_specs,
        out_specs=[tile] * 5,
        out_shape=[jax.ShapeDtypeStruct((b, s, d), F32)] * 2 + [jax.ShapeDtypeStruct((b, s, d), BF16)] * 3,
        compiler_params=_params(2),
    )(x, mod, gpre, w_qkv, *tables)


def _stack_halves(qh):
    lane = lax.broadcasted_iota(jnp.int32, qh.shape, 1)
    zero = jnp.zeros_like(qh)
    return jnp.concatenate([jnp.where(lane < LANES // 2, qh, zero),
                            jnp.where(lane >= LANES // 2, qh, zero)], axis=0)


def _softmax_block(q2, ks, vs, carry, mask=None):
    m, l, acc = carry
    s = _dot_nt(q2, ks)
    if mask is not None:
        s = jnp.where(mask, s, NEG_BIG)
    m_new = jnp.maximum(m, jnp.max(s, axis=-1, keepdims=True))
    alpha = jnp.exp(m - m_new)
    p = jnp.exp(s - m_new)
    l = alpha * l + jnp.sum(p, axis=-1, keepdims=True)
    acc = alpha * acc + _dot(p.astype(BF16), vs)
    return m_new, l, acc


def _softmax_init(rows):
    return (jnp.full((rows, 1), NEG_BIG, F32), jnp.zeros((rows, 1), F32), jnp.zeros((rows, LANES), F32))


def _diff_lambda(lam_ref, lam_init):
    lp = lam_ref[...]
    s1 = jnp.sum(lp[0:1] * lp[1:2], axis=-1, keepdims=True)
    s2 = jnp.sum(lp[2:3] * lp[3:4], axis=-1, keepdims=True)
    return jnp.exp(s1) - jnp.exp(s2) + lam_init


def _diff_head_out(carry, rows, lam, sg, lam_init):
    _, l, acc = carry
    o = acc[:rows] / l[:rows] - lam * (acc[rows:] / l[rows:])
    return (_rms(o, sg) * (1.0 - lam_init)).astype(BF16)


def _suffix_ones(n):
    row = lax.broadcasted_iota(jnp.int32, (n, n), 0)
    col = lax.broadcasted_iota(jnp.int32, (n, n), 1)
    return jnp.where(row > col, 1.0, 0.0).astype(BF16)


def _stick_block(q2, ks, vs, carry, ones_u, mask=None):
    run, acc = carry
    z = _dot_nt(q2, ks)
    log_beta = jnp.minimum(z, 0.0) - jnp.log1p(jnp.exp(-jnp.abs(z)))
    log_keep = log_beta - z
    if mask is not None:
        log_keep = jnp.where(mask, log_keep, 0.0)
    hi = log_keep.astype(BF16)
    lo = (log_keep - hi.astype(F32)).astype(BF16)
    after = _dot(hi, ones_u) + _dot(lo, ones_u)
    w = jnp.exp(log_beta + after + run)
    if mask is not None:
        w = jnp.where(mask, w, 0.0)
    acc = acc + _dot(w.astype(BF16), vs)
    run = run + jnp.sum(log_keep, axis=-1, keepdims=True)
    return run, acc


def _stick_init(rows):
    return jnp.zeros((rows, 1), F32), jnp.zeros((rows, LANES), F32)


def _stick_head_out(carry, rows):
    _, acc = carry
    lane = lax.broadcasted_iota(jnp.int32, (rows, LANES), 1)
    return jnp.where(lane < LANES // 2, acc[:rows], acc[rows:]).astype(BF16)


def _rel_index(tq, tk):
    qi = lax.broadcasted_iota(jnp.int32, (2 * tq, tk), 0)
    qi = jnp.where(qi >= tq, qi - tq, qi)
    ki = lax.broadcasted_iota(jnp.int32, (2 * tq, tk), 1)
    return qi, ki


def _attn_prompt_kernel(*refs, kind, tq, d, lam_init):
    if kind == "diff":
        x_ref, mod_ref, gpost_ref, q_ref, k_ref, v_ref, wout_ref, lam_ref, sg_ref, y_ref, obuf = refs
        lam = _diff_lambda(lam_ref, lam_init)
        sg = sg_ref[...]
    else:
        x_ref, mod_ref, gpost_ref, q_ref, k_ref, v_ref, wout_ref, y_ref, obuf = refs
        ones_u = _suffix_ones(tq)
    j = pl.program_id(1)
    qi, ki = _rel_index(tq, tq)
    if kind == "diff":
        diag_mask = (ki // CHUNK) <= (qi // CHUNK)
    else:
        diag_mask = ki < qi

    for h in range(d // LANES):
        sl = slice(h * LANES, (h + 1) * LANES)
        q2 = _stack_halves(q_ref[0, :, sl])

        def kv(kb):
            start = pl.multiple_of(kb * tq, tq)
            return k_ref[0, pl.ds(start, tq), sl], v_ref[0, pl.ds(start, tq), sl]

        if kind == "diff":
            carry = lax.fori_loop(0, j, lambda kb, c: _softmax_block(q2, *kv(kb), c), _softmax_init(2 * tq))
            carry = _softmax_block(q2, *kv(j), carry, diag_mask)
            obuf[:, sl] = _diff_head_out(carry, tq, lam, sg, lam_init)
        else:
            carry = _stick_block(q2, *kv(j), _stick_init(2 * tq), ones_u, diag_mask)
            carry = lax.fori_loop(0, j, lambda i, c: _stick_block(q2, *kv(j - 1 - i), c, ones_u), carry)
            obuf[:, sl] = _stick_head_out(carry, tq)

    m = _dot(obuf[...], wout_ref[...])
    y_ref[0] = x_ref[0] + mod_ref[0][2:3] * _rms(m, gpost_ref[...])


def _attn_prompt(kind, x, mod, gpost, qb, kb, vb, w_out, extra=(), lam_init=0.0):
    b, s, d = x.shape
    tq = min(ATTN_TILE, s)
    kern = functools.partial(_attn_prompt_kernel, kind=kind, tq=tq, d=d, lam_init=lam_init)
    tile = pl.BlockSpec((1, tq, d), lambda i, j: (i, j, 0))
    whole = pl.BlockSpec((1, s, d), lambda i, j: (i, 0, 0))
    return pl.pallas_call(
        kern,
        grid=(b, s // tq),
        in_specs=[tile, pl.BlockSpec((1, 6, d), lambda i, j: (i, 0, 0)), _const_spec((1, d)),
                  tile, whole, whole, _const_spec((d, d))] + [_const_spec(e.shape) for e in extra],
        out_specs=tile,
        out_shape=jax.ShapeDtypeStruct((b, s, d), F32),
        scratch_shapes=[pltpu.VMEM((tq, d), BF16)],
        compiler_params=_params(2),
    )(x, mod, gpost, qb, kb, vb, w_out, *extra)


def _attn_stream_kernel(*refs, kind, s_new, past, tkc, d, lam_init):
    if kind == "diff":
        (x_ref, mod_ref, gpost_ref, q_ref, kc_ref, vc_ref, kn_ref, vn_ref, wout_ref, lam_ref, sg_ref,
         y_ref, obuf) = refs
        lam = _diff_lambda(lam_ref, lam_init)
        sg = sg_ref[...]
    else:
        x_ref, mod_ref, gpost_ref, q_ref, kc_ref, vc_ref, kn_ref, vn_ref, wout_ref, y_ref, obuf = refs
        ones_new = _suffix_ones(kn_ref.shape[1])
        ones_cache = _suffix_ones(tkc)
    qi, ki = _rel_index(s_new, kn_ref.shape[1])
    if kind == "diff":
        new_mask = (((past + ki) // CHUNK) <= ((past + qi) // CHUNK)) & (ki < s_new)
    else:
        new_mask = (ki < qi) & (ki < s_new)
    n_cache = past // tkc

    for h in range(d // LANES):
        sl = slice(h * LANES, (h + 1) * LANES)
        q2 = _stack_halves(q_ref[0, :, sl])

        def cache_kv(kb):
            rows = slice(kb * tkc, (kb + 1) * tkc)
            return kc_ref[0, rows, sl].astype(BF16), vc_ref[0, rows, sl].astype(BF16)

        if kind == "diff":
            carry = _softmax_init(2 * s_new)
            for kb in range(n_cache):
                carry = _softmax_block(q2, *cache_kv(kb), carry)
            carry = _softmax_block(q2, kn_ref[0, :, sl], vn_ref[0, :, sl], carry, new_mask)
            obuf[:, sl] = _diff_head_out(carry, s_new, lam, sg, lam_init)
        else:
            carry = _stick_block(q2, kn_ref[0, :, sl], vn_ref[0, :, sl], _stick_init(2 * s_new), ones_new, new_mask)
            for kb in reversed(range(n_cache)):
                carry = _stick_block(q2, *cache_kv(kb), carry, ones_cache)
            obuf[:, sl] = _stick_head_out(carry, s_new)

    m = _dot(obuf[...], wout_ref[...])
    y_ref[0] = x_ref[0] + mod_ref[0][2:3] * _rms(m, gpost_ref[...])


def _attn_stream(kind, x, mod, gpost, qb, k_cache, v_cache, kb_new, vb_new, w_out, extra=(), lam_init=0.0):
    b, s, d = x.shape
    past = k_cache.shape[1]
    tkc = min(CACHE_TILE, past)
    pad = (-s) % LANES
    kn = jnp.pad(kb_new, ((0, 0), (0, pad), (0, 0)))
    vn = jnp.pad(vb_new, ((0, 0), (0, pad), (0, 0)))
    kern = functools.partial(_attn_stream_kernel, kind=kind, s_new=s, past=past, tkc=tkc, d=d, lam_init=lam_init)
    tile = pl.BlockSpec((1, s, d), lambda i: (i, 0, 0))
    cache = pl.BlockSpec((1, past, d), lambda i: (i, 0, 0))
    new = pl.BlockSpec((1, s + pad, d), lambda i: (i, 0, 0))
    return pl.pallas_call(
        kern,
        grid=(b,),
        in_specs=[tile, pl.BlockSpec((1, 6, d), lambda i: (i, 0, 0)), _const_spec((1, d)),
                  tile, cache, cache, new, new, _const_spec((d, d))] + [_const_spec(e.shape) for e in extra],
        out_specs=tile,
        out_shape=jax.ShapeDtypeStruct((b, s, d), F32),
        scratch_shapes=[pltpu.VMEM((s, d), BF16)],
        compiler_params=_params(1),
    )(x, mod, gpost, qb, k_cache, v_cache, kn, vn, w_out, *extra)


def _trunk(x, mod, pos, caches, wts):
    (norm_g, gm_w_in, gm_ln_g, gm_ln_b, gm_ws, gm_bs, gm_w_out,
     diff_w_qkv, diff_lambda, diff_subln_g, diff_w_out,
     sc_w_in, sc_conv_w, sc_w_out, sb_w_qkv, sb_w_out,
     ffn_w_in, ffn_conv_w, ffn_conv_b, ffn_w_out) = wts
    b, s, d = x.shape
    depth = norm_g.shape[0]
    dff = ffn_w_out.shape[1]
    gm_v, dk, dv, scs, sbk, sbv, ffs = [], [], [], [], [], [], []
    for i in range(depth):
        kind, j = i % N_MIXERS, i // N_MIXERS
        m_i = mod[i]
        g = [norm_g[i, n].reshape(1, d) for n in range(4)]
        if kind == 0:
            x, v_rows = _gmlp(x, m_i, g[0], g[1], gm_w_in[j], gm_ln_g[j], gm_ln_b[j], gm_ws[j], gm_bs[j],
                              gm_w_out[j])
            gm_v.append(v_rows)
        elif kind == 1:
            hd = d // (2 * DIFF_HEADS)
            lam_init = 0.8 - 0.6 * math.exp(-0.3 * i)
            k, v, qb, kb, vb = _qkv(x, m_i, g[0], diff_w_qkv[j], hd, rope=(pos, hd // 4))
            extra = (diff_lambda[j], diff_subln_g[j].reshape(1, 2 * hd))
            if caches is None:
                x = _attn_prompt("diff", x, m_i, g[1], qb, kb, vb, diff_w_out[j], extra, lam_init)
            else:
                kc = caches["diff_k"][j].reshape(b, -1, d)
                vc = caches["diff_v"][j].reshape(b, -1, d)
                x = _attn_stream("diff", x, m_i, g[1], qb, kc, vc, kb, vb, diff_w_out[j], extra, lam_init)
            dk.append(k.reshape(b, s, DIFF_HEADS, 2, hd))
            dv.append(v.reshape(b, s, DIFF_HEADS, 2 * hd))
        elif kind == 2:
            prev = jnp.zeros((b, 2, d), F32) if caches is None else caches["sconv"][j]
            x, st = _sconv(x, m_i, g[0], g[1], prev, sc_w_in[j], sc_conv_w[j], sc_w_out[j])
            scs.append(st)
        else:
            hd = d // SB_HEADS
            k, v, qb, kb, vb = _qkv(x, m_i, g[0], sb_w_qkv[j], hd)
            if caches is None:
                x = _attn_prompt("stick", x, m_i, g[1], qb, kb, vb, sb_w_out[j])
            else:
                kc = caches["sb_k"][j].reshape(b, -1, d)
                vc = caches["sb_v"][j].reshape(b, -1, d)
                x = _attn_stream("stick", x, m_i, g[1], qb, kc, vc, kb, vb, sb_w_out[j])
            sbk.append(k.reshape(b, s, SB_HEADS, hd))
            sbv.append(v.reshape(b, s, SB_HEADS, hd))
        prev = jnp.zeros((b, 2, dff), F32) if caches is None else caches["ffn"][i]
        x, st = _ffn(x, m_i, g[2], g[3], prev, ffn_w_in[i], ffn_conv_w[i], ffn_conv_b[i], ffn_w_out[i])
        ffs.append(st)
    return (x, jnp.stack(gm_v), jnp.stack(dk), jnp.stack(dv), jnp.stack(scs),
            jnp.stack(sbk), jnp.stack(sbv), jnp.stack(ffs))


def kernel(x_prompt, x_sample, cache_diff_k, cache_diff_v, state_sconv, cache_sb_k, cache_sb_v, state_ffn_conv, c_prompt, c_sample, ada_w, ada_b, norm_g, gm_w_in, gm_ln_g, gm_ln_b, gm_ws, gm_bs, gm_w_out, diff_w_qkv, diff_lambda, diff_subln_g, diff_w_out, sc_w_in, sc_conv_w, sc_w_out, sb_w_qkv, sb_w_out, ffn_w_in, ffn_conv_w, ffn_conv_b, ffn_w_out):
    bp = x_prompt.shape[0]
    past = cache_diff_k.shape[2]
    cast = lambda w: w.astype(BF16)
    wts = (norm_g, cast(gm_w_in), gm_ln_g, gm_ln_b, gm_ws, gm_bs, cast(gm_w_out),
           cast(diff_w_qkv), diff_lambda, diff_subln_g, cast(diff_w_out),
           cast(sc_w_in), sc_conv_w, cast(sc_w_out), cast(sb_w_qkv), cast(sb_w_out),
           cast(ffn_w_in), ffn_conv_w, ffn_conv_b, cast(ffn_w_out))

    mod = _modulation(jnp.concatenate([c_prompt, c_sample], axis=0), ada_w, ada_b)
    mod = jnp.transpose(mod, (0, 2, 1, 3))
    mod_p, mod_s = mod[:, :bp], mod[:, bp:]

    pos_p = jnp.arange(x_prompt.shape[1])
    pos_s = past + jnp.arange(x_sample.shape[1])
    caches = {"diff_k": cache_diff_k, "diff_v": cache_diff_v, "sconv": state_sconv,
              "sb_k": cache_sb_k, "sb_v": cache_sb_v, "ffn": state_ffn_conv}

    (y_p, _, dk_p, dv_p, sc_p, sbk_p, sbv_p, ff_p) = _trunk(x_prompt, mod_p, pos_p, None, wts)
    (y_s, gmv_s, dk_s, dv_s, sc_s, sbk_s, sbv_s, ff_s) = _trunk(x_sample, mod_s, pos_s, caches, wts)
    return (y_p, y_s, gmv_s, dk_p, dv_p, dk_s, dv_s, sc_p, sc_s,
            sbk_p, sbv_p, sbk_s, sbv_s, ff_p, ff_s)
```

```python
import functools
import math

import jax
import jax.numpy as jnp
from jax import lax
from jax.experimental import pallas as pl
from jax.experimental.pallas import tpu as pltpu

F32 = jnp.float32
BF16 = jnp.bfloat16

EPS = 1e-6
CHUNK = 64
N_MIXERS = 4
GM_CHUNK = 128
GM_GROUPS = 8
DIFF_HEADS = 8
ROPE_THETA = 500000.0
SB_HEADS = 16

LANES = 128
CARRY_ROWS = 8
NEG_BIG = -1e30
LOG2E = math.log2(math.e)

ROW_TILE = 512
ATTN_TILE = 256
CACHE_TILE = 512
CONV_COLS = 256
HEADS_PER_LOOP = 8
VMEM_LIMIT = 56 * 1024 * 1024


def _params(n_axes):
    return pltpu.CompilerParams(dimension_semantics=("arbitrary",) * n_axes,
                                vmem_limit_bytes=VMEM_LIMIT)


def _const_spec(shape):
    zeros = (0,) * len(shape)
    return pl.BlockSpec(shape, lambda *_: zeros, pipeline_mode=pl.Buffered(1))


def _rms(x, g):
    return x * lax.rsqrt(jnp.mean(x * x, axis=-1, keepdims=True) + EPS) * g


def _prenorm(x, g, scale, shift):
    return _rms(x, g) * (1.0 + scale) + shift


def _dot(a, b):
    return jnp.dot(a, b, preferred_element_type=F32)


def _dot_nt(a, b):
    return lax.dot_general(a, b, (((1,), (1,)), ((), ())), preferred_element_type=F32)


def _mod_kernel(c_ref, w_ref, b_ref, o_ref):
    c = c_ref[...]
    cs = (c * jax.nn.sigmoid(c)).astype(BF16)
    o_ref[0] = _dot(cs, w_ref[0].astype(BF16)) + b_ref[0]


def _modulation(c_all, ada_w, ada_b):
    depth, d, _ = ada_w.shape
    rows = c_all.shape[0]
    out = pl.pallas_call(
        _mod_kernel,
        grid=(depth, 6),
        in_specs=[
            pl.BlockSpec((rows, d), lambda i, j: (0, 0)),
            pl.BlockSpec((1, d, d), lambda i, j: (i, 0, j)),
            pl.BlockSpec((1, 1, d), lambda i, j: (i * 6 + j, 0, 0)),
        ],
        out_specs=pl.BlockSpec((1, rows, d), lambda i, j: (i * 6 + j, 0, 0)),
        out_shape=jax.ShapeDtypeStruct((depth * 6, rows, d), F32),
        compiler_params=_params(2),
    )(c_all, ada_w, ada_b.reshape(depth * 6, 1, d))
    return out.reshape(depth, 6, rows, d)


def _conv3(val, cw, carry, buf, lo, width, tm):
    buf[0:CARRY_ROWS, :] = carry[:, lo:lo + width]
    buf[CARRY_ROWS:CARRY_ROWS + tm, :] = val
    back1 = buf[CARRY_ROWS - 1:CARRY_ROWS - 1 + tm, :]
    back2 = buf[CARRY_ROWS - 2:CARRY_ROWS - 2 + tm, :]
    carry[:, lo:lo + width] = buf[tm:tm + CARRY_ROWS, :]
    return cw[2:3] * val + cw[1:2] * back1 + cw[0:1] * back2


def _init_carry(carry, prev_ref):
    @pl.when(pl.program_id(1) == 0)
    def _():
        carry[...] = jnp.zeros_like(carry)
        carry[CARRY_ROWS - 2:CARRY_ROWS, :] = prev_ref[0]


def _ffn_kernel(x_ref, mod_ref, gpre_ref, gpost_ref, prev_ref, win_ref, cw_ref, cb_ref, wout_ref,
                y_ref, st_ref, carry, buf, *, tm, dff, fc):
    _init_carry(carry, prev_ref)
    x = x_ref[0]
    mod = mod_ref[0]
    h = _prenorm(x, gpre_ref[...], mod[4:5], mod[3:4]).astype(BF16)
    acc = jnp.zeros(x.shape, F32)
    for c in range(dff // fc):
        lo = c * fc
        g = _dot(h, win_ref[:, lo:lo + fc])
        u = _dot(h, win_ref[:, dff + lo:dff + lo + fc])
        y = _conv3(g, cw_ref[:, lo:lo + fc], carry, buf, lo, fc, tm) + cb_ref[:, lo:lo + fc]
        a = (y * jax.nn.sigmoid(y) * u).astype(BF16)
        acc = acc + _dot(a, wout_ref[lo:lo + fc, :])
    st_ref[0] = carry[CARRY_ROWS - 2:CARRY_ROWS, :]
    y_ref[0] = x + mod[5:6] * _rms(acc, gpost_ref[...])


def _ffn(x, mod, gpre, gpost, prev, w_in, conv_w, conv_b, w_out):
    b, s, d = x.shape
    dff = w_out.shape[0]
    tm = min(ROW_TILE, s)
    fc = CONV_COLS
    kern = functools.partial(_ffn_kernel, tm=tm, dff=dff, fc=fc)
    return pl.pallas_call(
        kern,
        grid=(b, s // tm),
        in_specs=[
            pl.BlockSpec((1, tm, d), lambda i, j: (i, j, 0)),
            pl.BlockSpec((1, 6, d), lambda i, j: (i, 0, 0)),
            _const_spec((1, d)),
            _const_spec((1, d)),
            pl.BlockSpec((1, 2, dff), lambda i, j: (i, 0, 0)),
            _const_spec((d, 2 * dff)),
            _const_spec((3, dff)),
            _const_spec((1, dff)),
            _const_spec((dff, d)),
        ],
        out_specs=[
            pl.BlockSpec((1, tm, d), lambda i, j: (i, j, 0)),
            pl.BlockSpec((1, 2, dff), lambda i, j: (i, 0, 0)),
        ],
        out_shape=[
            jax.ShapeDtypeStruct((b, s, d), F32),
            jax.ShapeDtypeStruct((b, 2, dff), F32),
        ],
        scratch_shapes=[
            pltpu.VMEM((CARRY_ROWS, dff), F32),
            pltpu.VMEM((CARRY_ROWS + tm, fc), F32),
        ],
        compiler_params=_params(2),
    )(x, mod, gpre, gpost, prev, w_in, conv_w, conv_b.reshape(1, dff), w_out)


def _sconv_kernel(x_ref, mod_ref, gpre_ref, gpost_ref, prev_ref, win_ref, cw_ref, wout_ref,
                  y_ref, st_ref, carry, buf, *, tm, dsc, fc):
    _init_carry(carry, prev_ref)
    x = x_ref[0]
    mod = mod_ref[0]
    h = _prenorm(x, gpre_ref[...], mod[1:2], mod[0:1]).astype(BF16)
    acc = jnp.zeros(x.shape, F32)
    for c in range(dsc // fc):
        lo = c * fc
        b_gate = _dot(h, win_ref[:, lo:lo + fc])
        c_gate = _dot(h, win_ref[:, dsc + lo:dsc + lo + fc])
        xin = _dot(h, win_ref[:, 2 * dsc + lo:2 * dsc + lo + fc])
        y = _conv3(c_gate * xin, cw_ref[:, lo:lo + fc], carry, buf, lo, fc, tm)
        acc = acc + _dot((b_gate * y).astype(BF16), wout_ref[lo:lo + fc, :])
    st_ref[0] = carry[CARRY_ROWS - 2:CARRY_ROWS, :]
    y_ref[0] = x + mod[2:3] * _rms(acc, gpost_ref[...])


def _sconv(x, mod, gpre, gpost, prev, w_in, conv_w, w_out):
    b, s, d = x.shape
    dsc = w_out.shape[0]
    tm = min(ROW_TILE, s)
    fc = CONV_COLS
    kern = functools.partial(_sconv_kernel, tm=tm, dsc=dsc, fc=fc)
    return pl.pallas_call(
        kern,
        grid=(b, s // tm),
        in_specs=[
            pl.BlockSpec((1, tm, d), lambda i, j: (i, j, 0)),
            pl.BlockSpec((1, 6, d), lambda i, j: (i, 0, 0)),
            _const_spec((1, d)),
            _const_spec((1, d)),
            pl.BlockSpec((1, 2, dsc), lambda i, j: (i, 0, 0)),
            _const_spec((d, 3 * dsc)),
            _const_spec((3, dsc)),
            _const_spec((dsc, d)),
        ],
        out_specs=[
            pl.BlockSpec((1, tm, d), lambda i, j: (i, j, 0)),
            pl.BlockSpec((1, 2, dsc), lambda i, j: (i, 0, 0)),
        ],
        out_shape=[
            jax.ShapeDtypeStruct((b, s, d), F32),
            jax.ShapeDtypeStruct((b, 2, dsc), F32),
        ],
        scratch_shapes=[
            pltpu.VMEM((CARRY_ROWS, dsc), F32),
            pltpu.VMEM((CARRY_ROWS + tm, fc), F32),
        ],
        compiler_params=_params(2),
    )(x, mod, gpre, gpost, prev, w_in, conv_w, w_out)


def _gmlp_kernel(x_ref, mod_ref, gpre_ref, gpost_ref, win_ref, lng_ref, lnb_ref, ws_ref, bsf_ref,
                 wout_ref, y_ref, v_ref, mixbuf, *, tm, span, dgm):
    x = x_ref[0]
    mod = mod_ref[0]
    h = _prenorm(x, gpre_ref[...], mod[1:2], mod[0:1]).astype(BF16)
    u = jax.nn.gelu(_dot(h, win_ref[:, 0:dgm]))
    v = jax.nn.gelu(_dot(h, win_ref[:, dgm:2 * dgm]))
    vc = v - jnp.mean(v, axis=-1, keepdims=True)
    v = vc * lax.rsqrt(jnp.mean(vc * vc, axis=-1, keepdims=True) + EPS) * lng_ref[...] + lnb_ref[...]
    v_ref[0] = v
    mix_dtype = BF16 if span % 16 == 0 and span >= LANES else F32
    vm = v.astype(mix_dtype)
    gw = dgm // GM_GROUPS
    row = lax.broadcasted_iota(jnp.int32, (span, span), 0)
    col = lax.broadcasted_iota(jnp.int32, (span, span), 1)
    for g in range(GM_GROUPS):
        w = jnp.where(row >= col, ws_ref[g], 0.0).astype(mix_dtype)
        for r in range(tm // span):
            mixbuf[r * span:(r + 1) * span, g * gw:(g + 1) * gw] = (
                _dot(w, vm[r * span:(r + 1) * span, g * gw:(g + 1) * gw]) + bsf_ref[:, g * gw:(g + 1) * gw])
    m = _dot((u * mixbuf[...]).astype(BF16), wout_ref[...])
    y_ref[0] = x + mod[2:3] * _rms(m, gpost_ref[...])


def _gmlp(x, mod, gpre, gpost, w_in, ln_g, ln_b, ws, bs, w_out):
    b, s, d = x.shape
    dgm = w_out.shape[0]
    span = GM_CHUNK if s >= GM_CHUNK else s
    tm = min(ROW_TILE, s)
    ws_l = ws[:, :span, :span]
    bs_tile = jnp.repeat(bs[:, :span].T, dgm // GM_GROUPS, axis=1)
    kern = functools.partial(_gmlp_kernel, tm=tm, span=span, dgm=dgm)
    return pl.pallas_call(
        kern,
        grid=(b, s // tm),
        in_specs=[
            pl.BlockSpec((1, tm, d), lambda i, j: (i, j, 0)),
            pl.BlockSpec((1, 6, d), lambda i, j: (i, 0, 0)),
            _const_spec((1, d)),
            _const_spec((1, d)),
            _const_spec((d, 2 * dgm)),
            _const_spec((1, dgm)),
            _const_spec((1, dgm)),
            _const_spec((GM_GROUPS, span, span)),
            _const_spec((span, dgm)),
            _const_spec((dgm, d)),
        ],
        out_specs=[
            pl.BlockSpec((1, tm, d), lambda i, j: (i, j, 0)),
            pl.BlockSpec((1, tm, dgm), lambda i, j: (i, j, 0)),
        ],
        out_shape=[
            jax.ShapeDtypeStruct((b, s, d), F32),
            jax.ShapeDtypeStruct((b, s, dgm), F32),
        ],
        scratch_shapes=[pltpu.VMEM((tm, dgm), F32)],
        compiler_params=_params(2),
    )(x, mod, gpre, gpost, w_in, ln_g.reshape(1, dgm), ln_b.reshape(1, dgm), ws_l, bs_tile, w_out)


def _rotate(t, cos, sin_fwd, sin_bwd, rot_half):
    outs = []
    for s in range(t.shape[1] // LANES):
        ts = t[:, s * LANES:(s + 1) * LANES]
        ahead = pltpu.roll(ts, LANES - rot_half, 1)
        behind = pltpu.roll(ts, rot_half, 1)
        outs.append(ts * cos + ahead * sin_fwd + behind * sin_bwd)
    return jnp.concatenate(outs, axis=1)


def _qkv_kernel(*refs, d, qscale, rot_half):
    if rot_half:
        (x_ref, mod_ref, gpre_ref, w_ref, cos_ref, sf_ref, sb_ref,
         k_ref, v_ref, qb_ref, kb_ref, vb_ref) = refs
    else:
        x_ref, mod_ref, gpre_ref, w_ref, k_ref, v_ref, qb_ref, kb_ref, vb_ref = refs
    x = x_ref[0]
    mod = mod_ref[0]
    h = _prenorm(x, gpre_ref[...], mod[1:2], mod[0:1]).astype(BF16)
    q = _dot(h, w_ref[:, 0:d])
    k = _dot(h, w_ref[:, d:2 * d])
    v = _dot(h, w_ref[:, 2 * d:3 * d])
    if rot_half:
        cos, sf, sb = cos_ref[...], sf_ref[...], sb_ref[...]
        q = _rotate(q, cos, sf, sb, rot_half)
        k = _rotate(k, cos, sf, sb, rot_half)
    k_ref[0] = k
    v_ref[0] = v
    qb_ref[0] = (q * qscale).astype(BF16)
    kb_ref[0] = k.astype(BF16)
    vb_ref[0] = v.astype(BF16)


def _rope_tables(pos, head_dim, rot_dim):
    half = rot_dim // 2
    inv = ROPE_THETA ** (-jnp.arange(0, rot_dim, 2, dtype=F32) / rot_dim)
    ang = pos.astype(F32)[:, None] * inv[None, :]
    cos, sin = jnp.cos(ang), jnp.sin(ang)
    n = pos.shape[0]
    pad = jnp.zeros((n, head_dim - rot_dim), F32)
    zeros = jnp.zeros((n, half), F32)
    cos_t = jnp.concatenate([cos, cos, pad + 1.0], axis=1)
    sf_t = jnp.concatenate([-sin, zeros, pad], axis=1)
    sb_t = jnp.concatenate([zeros, sin, pad], axis=1)
    rep = LANES // head_dim
    return tuple(jnp.tile(t, (1, rep)) for t in (cos_t, sf_t, sb_t))


def _qkv(x, mod, gpre, w_qkv, head_dim, rope=None):
    b, s, d = x.shape
    tm = min(ROW_TILE, s)
    rot_half = 0
    tables = ()
    table_specs = []
    if rope is not None:
        pos, rot_dim = rope
        rot_half = rot_dim // 2
        tables = _rope_tables(pos, head_dim, rot_dim)
        table_specs = [pl.BlockSpec((tm, LANES), lambda i, j: (j, 0))] * 3
    kern = functools.partial(_qkv_kernel, d=d, qscale=head_dim ** -0.5 * LOG2E, rot_half=rot_half)
    tile = pl.BlockSpec((1, tm, d), lambda i, j: (i, j, 0))
    return pl.pallas_call(
        kern,
        grid=(b, s // tm),
        in_specs=[tile, pl.BlockSpec((1, 6, d), lambda i, j: (i, 0, 0)), _const_spec((1, d)),
                  _const_spec((d, 3 * d))] + table_specs,
        out_specs=[tile] * 5,
        out_shape=[jax.ShapeDtypeStruct((b, s, d), F32)] * 2 + [jax.ShapeDtypeStruct((b, s, d), BF16)] * 3,
        compiler_params=_params(2),
    )(x, mod, gpre, w_qkv, *tables)


def _stack_halves(qh):
    lane = lax.broadcasted_iota(jnp.int32, qh.shape, 1)
    zero = jnp.zeros_like(qh)
    return jnp.concatenate([jnp.where(lane < LANES // 2, qh, zero),
                            jnp.where(lane >= LANES // 2, qh, zero)], axis=0)


def _lane_tiles(x):
    return [x[:, t * LANES:(t + 1) * LANES] for t in range(x.shape[1] // LANES)]


def _row_bcast(col):
    return jnp.broadcast_to(col, (col.shape[0], LANES))


def _softmax_init(state, h):
    m_ref, l_ref, acc_ref = state
    m_ref[h] = jnp.full(m_ref.shape[1:], NEG_BIG, F32)
    l_ref[h] = jnp.zeros(l_ref.shape[1:], F32)
    acc_ref[h] = jnp.zeros(acc_ref.shape[1:], F32)


def _softmax_block(q2, ks, vs, state, h, mask=None):
    m_ref, l_ref, acc_ref = state
    s = _dot_nt(q2, ks)
    if mask is not None:
        s = jnp.where(mask, s, NEG_BIG)
    tiles = _lane_tiles(s)
    m_old = m_ref[h]
    m_blk = functools.reduce(jnp.maximum, tiles)
    m_new = jnp.maximum(m_old, _row_bcast(jnp.max(m_blk, axis=-1, keepdims=True)))
    alpha = jnp.exp2(m_old - m_new)
    ps = [jnp.exp2(t - m_new) for t in tiles]
    p_sum = functools.reduce(jnp.add, ps)
    m_ref[h] = m_new
    l_ref[h] = alpha * l_ref[h] + p_sum
    acc_ref[h] = alpha * acc_ref[h] + _dot(jnp.concatenate(ps, axis=1).astype(BF16), vs)


def _diff_lambda(lam_ref, lam_init):
    lp = lam_ref[...]
    s1 = jnp.sum(lp[0:1] * lp[1:2], axis=-1, keepdims=True)
    s2 = jnp.sum(lp[2:3] * lp[3:4], axis=-1, keepdims=True)
    return jnp.exp(s1) - jnp.exp(s2) + lam_init


def _diff_head_out(state, h, rows, lam, sg, lam_init):
    _, l_ref, acc_ref = state
    l, acc = jnp.sum(l_ref[h], axis=-1, keepdims=True), acc_ref[h]
    o = acc[:rows] / l[:rows] - lam * (acc[rows:] / l[rows:])
    return (_rms(o, sg) * (1.0 - lam_init)).astype(BF16)


def _suffix_ones(n):
    row = lax.broadcasted_iota(jnp.int32, (n, n), 0)
    col = lax.broadcasted_iota(jnp.int32, (n, n), 1)
    return jnp.where(row > col, 1.0, 0.0).astype(BF16)


def _stick_init(state, h):
    run_ref, acc_ref = state
    run_ref[h] = jnp.zeros(run_ref.shape[1:], F32)
    acc_ref[h] = jnp.zeros(acc_ref.shape[1:], F32)


def _stick_block(q2, ks, vs, state, h, ones_u, mask=None):
    run_ref, acc_ref = state
    z = _dot_nt(q2, ks)
    log_beta = jnp.minimum(z, 0.0) - jnp.log2(1.0 + jnp.exp2(-jnp.abs(z)))
    log_keep = log_beta - z
    if mask is not None:
        log_keep = jnp.where(mask, log_keep, 0.0)
    after = _dot(log_keep.astype(BF16), ones_u)
    run = run_ref[h]
    w = jnp.exp2(log_beta + after + jnp.concatenate([run] * (z.shape[1] // LANES), axis=1))
    if mask is not None:
        w = jnp.where(mask, w, 0.0)
    acc_ref[h] = acc_ref[h] + _dot(w.astype(BF16), vs)
    keep_sum = functools.reduce(jnp.add, _lane_tiles(log_keep))
    run_ref[h] = run + _row_bcast(jnp.sum(keep_sum, axis=-1, keepdims=True))


def _stick_head_out(state, h, rows):
    acc = state[1][h]
    lane = lax.broadcasted_iota(jnp.int32, (rows, LANES), 1)
    return jnp.where(lane < LANES // 2, acc[:rows], acc[rows:]).astype(BF16)


def _attn_state(kind, n_slabs, rows):
    n = 3 if kind == "diff" else 2
    return [pltpu.VMEM((n_slabs, rows, LANES), F32)] * n


def _rel_index(tq, tk):
    qi = lax.broadcasted_iota(jnp.int32, (2 * tq, tk), 0)
    qi = jnp.where(qi >= tq, qi - tq, qi)
    ki = lax.broadcasted_iota(jnp.int32, (2 * tq, tk), 1)
    return qi, ki


def _attn_prompt_kernel(*refs, kind, tq, d, lam_init):
    if kind == "diff":
        x_ref, mod_ref, gpost_ref, q_ref, k_ref, v_ref, wout_ref, lam_ref, sg_ref, y_ref, obuf, *state = refs
        lam = _diff_lambda(lam_ref, lam_init)
        sg = sg_ref[...]
    else:
        x_ref, mod_ref, gpost_ref, q_ref, k_ref, v_ref, wout_ref, y_ref, obuf, *state = refs
        ones_u = _suffix_ones(tq)
    j = pl.program_id(1)
    qi, ki = _rel_index(tq, tq)
    if kind == "diff":
        diag_mask = (ki // CHUNK) <= (qi // CHUNK)
    else:
        diag_mask = ki < qi

    def kv(kb, sl):
        start = pl.multiple_of(kb * tq, tq)
        return k_ref[0, pl.ds(start, tq), sl], v_ref[0, pl.ds(start, tq), sl]

    n_slabs = d // LANES
    for h0 in range(0, n_slabs, HEADS_PER_LOOP):
        heads = range(h0, min(h0 + HEADS_PER_LOOP, n_slabs))
        slabs = {h: slice(h * LANES, (h + 1) * LANES) for h in heads}
        q2s = {h: _stack_halves(q_ref[0, :, slabs[h]]) for h in heads}
        if kind == "diff":
            for h in heads:
                _softmax_init(state, h)

            @pl.loop(0, j)
            def _(kb):
                for h in heads:
                    _softmax_block(q2s[h], *kv(kb, slabs[h]), state, h)

            for h in heads:
                _softmax_block(q2s[h], *kv(j, slabs[h]), state, h, diag_mask)
                obuf[:, slabs[h]] = _diff_head_out(state, h, tq, lam, sg, lam_init)
        else:
            for h in heads:
                _stick_init(state, h)
                _stick_block(q2s[h], *kv(j, slabs[h]), state, h, ones_u, diag_mask)

            @pl.loop(0, j)
            def _(i):
                for h in heads:
                    _stick_block(q2s[h], *kv(j - 1 - i, slabs[h]), state, h, ones_u)

            for h in heads:
                obuf[:, slabs[h]] = _stick_head_out(state, h, tq)

    m = _dot(obuf[...], wout_ref[...])
    y_ref[0] = x_ref[0] + mod_ref[0][2:3] * _rms(m, gpost_ref[...])


def _attn_prompt(kind, x, mod, gpost, qb, kb, vb, w_out, extra=(), lam_init=0.0):
    b, s, d = x.shape
    tq = min(ATTN_TILE, s)
    kern = functools.partial(_attn_prompt_kernel, kind=kind, tq=tq, d=d, lam_init=lam_init)
    tile = pl.BlockSpec((1, tq, d), lambda i, j: (i, j, 0))
    whole = pl.BlockSpec((1, s, d), lambda i, j: (i, 0, 0))
    return pl.pallas_call(
        kern,
        grid=(b, s // tq),
        in_specs=[tile, pl.BlockSpec((1, 6, d), lambda i, j: (i, 0, 0)), _const_spec((1, d)),
                  tile, whole, whole, _const_spec((d, d))] + [_const_spec(e.shape) for e in extra],
        out_specs=tile,
        out_shape=jax.ShapeDtypeStruct((b, s, d), F32),
        scratch_shapes=[pltpu.VMEM((tq, d), BF16)] + _attn_state(kind, d // LANES, 2 * tq),
        compiler_params=_params(2),
    )(x, mod, gpost, qb, kb, vb, w_out, *extra)


def _attn_stream_kernel(*refs, kind, s_new, past, tkc, d, lam_init):
    if kind == "diff":
        (x_ref, mod_ref, gpost_ref, q_ref, kc_ref, vc_ref, kn_ref, vn_ref, wout_ref, lam_ref, sg_ref,
         y_ref, obuf, *state) = refs
        lam = _diff_lambda(lam_ref, lam_init)
        sg = sg_ref[...]
    else:
        x_ref, mod_ref, gpost_ref, q_ref, kc_ref, vc_ref, kn_ref, vn_ref, wout_ref, y_ref, obuf, *state = refs
        ones_new = _suffix_ones(kn_ref.shape[1])
        ones_cache = _suffix_ones(tkc)
    qi, ki = _rel_index(s_new, kn_ref.shape[1])
    if kind == "diff":
        new_mask = (((past + ki) // CHUNK) <= ((past + qi) // CHUNK)) & (ki < s_new)
    else:
        new_mask = (ki < qi) & (ki < s_new)
    n_cache = past // tkc

    for h in range(d // LANES):
        sl = slice(h * LANES, (h + 1) * LANES)
        q2 = _stack_halves(q_ref[0, :, sl])

        def cache_kv(kb):
            rows = slice(kb * tkc, (kb + 1) * tkc)
            return kc_ref[0, rows, sl].astype(BF16), vc_ref[0, rows, sl].astype(BF16)

        if kind == "diff":
            _softmax_init(state, h)
            for kb in range(n_cache):
                _softmax_block(q2, *cache_kv(kb), state, h)
            _softmax_block(q2, kn_ref[0, :, sl], vn_ref[0, :, sl], state, h, new_mask)
            obuf[:, sl] = _diff_head_out(state, h, s_new, lam, sg, lam_init)
        else:
            _stick_init(state, h)
            _stick_block(q2, kn_ref[0, :, sl], vn_ref[0, :, sl], state, h, ones_new, new_mask)
            for kb in reversed(range(n_cache)):
                _stick_block(q2, *cache_kv(kb), state, h, ones_cache)
            obuf[:, sl] = _stick_head_out(state, h, s_new)

    m = _dot(obuf[...], wout_ref[...])
    y_ref[0] = x_ref[0] + mod_ref[0][2:3] * _rms(m, gpost_ref[...])


def _attn_stream(kind, x, mod, gpost, qb, k_cache, v_cache, kb_new, vb_new, w_out, extra=(), lam_init=0.0):
    b, s, d = x.shape
    past = k_cache.shape[1]
    tkc = min(CACHE_TILE, past)
    pad = (-s) % LANES
    kn = jnp.pad(kb_new, ((0, 0), (0, pad), (0, 0)))
    vn = jnp.pad(vb_new, ((0, 0), (0, pad), (0, 0)))
    kern = functools.partial(_attn_stream_kernel, kind=kind, s_new=s, past=past, tkc=tkc, d=d, lam_init=lam_init)
    tile = pl.BlockSpec((1, s, d), lambda i: (i, 0, 0))
    cache = pl.BlockSpec((1, past, d), lambda i: (i, 0, 0))
    new = pl.BlockSpec((1, s + pad, d), lambda i: (i, 0, 0))
    return pl.pallas_call(
        kern,
        grid=(b,),
        in_specs=[tile, pl.BlockSpec((1, 6, d), lambda i: (i, 0, 0)), _const_spec((1, d)),
                  tile, cache, cache, new, new, _const_spec((d, d))] + [_const_spec(e.shape) for e in extra],
        out_specs=tile,
        out_shape=jax.ShapeDtypeStruct((b, s, d), F32),
        scratch_shapes=[pltpu.VMEM((s, d), BF16)] + _attn_state(kind, d // LANES, 2 * s),
        compiler_params=_params(1),
    )(x, mod, gpost, qb, k_cache, v_cache, kn, vn, w_out, *extra)


def _trunk(x, mod, pos, caches, wts):
    (norm_g, gm_w_in, gm_ln_g, gm_ln_b, gm_ws, gm_bs, gm_w_out,
     diff_w_qkv, diff_lambda, diff_subln_g, diff_w_out,
     sc_w_in, sc_conv_w, sc_w_out, sb_w_qkv, sb_w_out,
     ffn_w_in, ffn_conv_w, ffn_conv_b, ffn_w_out) = wts
    b, s, d = x.shape
    depth = norm_g.shape[0]
    dff = ffn_w_out.shape[1]
    gm_v, dk, dv, scs, sbk, sbv, ffs = [], [], [], [], [], [], []
    for i in range(depth):
        kind, j = i % N_MIXERS, i // N_MIXERS
        m_i = mod[i]
        g = [norm_g[i, n].reshape(1, d) for n in range(4)]
        if kind == 0:
            x, v_rows = _gmlp(x, m_i, g[0], g[1], gm_w_in[j], gm_ln_g[j], gm_ln_b[j], gm_ws[j], gm_bs[j],
                              gm_w_out[j])
            gm_v.append(v_rows)
        elif kind == 1:
            hd = d // (2 * DIFF_HEADS)
            lam_init = 0.8 - 0.6 * math.exp(-0.3 * i)
            k, v, qb, kb, vb = _qkv(x, m_i, g[0], diff_w_qkv[j], hd, rope=(pos, hd // 4))
            extra = (diff_lambda[j], diff_subln_g[j].reshape(1, 2 * hd))
            if caches is None:
                x = _attn_prompt("diff", x, m_i, g[1], qb, kb, vb, diff_w_out[j], extra, lam_init)
            else:
                kc = caches["diff_k"][j].reshape(b, -1, d)
                vc = caches["diff_v"][j].reshape(b, -1, d)
                x = _attn_stream("diff", x, m_i, g[1], qb, kc, vc, kb, vb, diff_w_out[j], extra, lam_init)
            dk.append(k.reshape(b, s, DIFF_HEADS, 2, hd))
            dv.append(v.reshape(b, s, DIFF_HEADS, 2 * hd))
        elif kind == 2:
            prev = jnp.zeros((b, 2, d), F32) if caches is None else caches["sconv"][j]
            x, st = _sconv(x, m_i, g[0], g[1], prev, sc_w_in[j], sc_conv_w[j], sc_w_out[j])
            scs.append(st)
        else:
            hd = d // SB_HEADS
            k, v, qb, kb, vb = _qkv(x, m_i, g[0], sb_w_qkv[j], hd)
            if caches is None:
                x = _attn_prompt("stick", x, m_i, g[1], qb, kb, vb, sb_w_out[j])
            else:
                kc = caches["sb_k"][j].reshape(b, -1, d)
                vc = caches["sb_v"][j].reshape(b, -1, d)
                x = _attn_stream("stick", x, m_i, g[1], qb, kc, vc, kb, vb, sb_w_out[j])
            sbk.append(k.reshape(b, s, SB_HEADS, hd))
            sbv.append(v.reshape(b, s, SB_HEADS, hd))
        prev = jnp.zeros((b, 2, dff), F32) if caches is None else caches["ffn"][i]
        x, st = _ffn(x, m_i, g[2], g[3], prev, ffn_w_in[i], ffn_conv_w[i], ffn_conv_b[i], ffn_w_out[i])
        ffs.append(st)
    return (x, jnp.stack(gm_v), jnp.stack(dk), jnp.stack(dv), jnp.stack(scs),
            jnp.stack(sbk), jnp.stack(sbv), jnp.stack(ffs))


def kernel(x_prompt, x_sample, cache_diff_k, cache_diff_v, state_sconv, cache_sb_k, cache_sb_v, state_ffn_conv, c_prompt, c_sample, ada_w, ada_b, norm_g, gm_w_in, gm_ln_g, gm_ln_b, gm_ws, gm_bs, gm_w_out, diff_w_qkv, diff_lambda, diff_subln_g, diff_w_out, sc_w_in, sc_conv_w, sc_w_out, sb_w_qkv, sb_w_out, ffn_w_in, ffn_conv_w, ffn_conv_b, ffn_w_out):
    bp = x_prompt.shape[0]
    past = cache_diff_k.shape[2]
    cast = lambda w: w.astype(BF16)
    wts = (norm_g, cast(gm_w_in), gm_ln_g, gm_ln_b, gm_ws, gm_bs, cast(gm_w_out),
           cast(diff_w_qkv), diff_lambda, diff_subln_g, cast(diff_w_out),
           cast(sc_w_in), sc_conv_w, cast(sc_w_out), cast(sb_w_qkv), cast(sb_w_out),
           cast(ffn_w_in), ffn_conv_w, ffn_conv_b, cast(ffn_w_out))

    mod = _modulation(jnp.concatenate([c_prompt, c_sample], axis=0), ada_w, ada_b)
    mod = jnp.transpose(mod, (0, 2, 1, 3))
    mod_p, mod_s = mod[:, :bp], mod[:, bp:]

    pos_p = jnp.arange(x_prompt.shape[1])
    pos_s = past + jnp.arange(x_sample.shape[1])
    caches = {"diff_k": cache_diff_k, "diff_v": cache_diff_v, "sconv": state_sconv,
              "sb_k": cache_sb_k, "sb_v": cache_sb_v, "ffn": state_ffn_conv}

    (y_p, _, dk_p, dv_p, sc_p, sbk_p, sbv_p, ff_p) = _trunk(x_prompt, mod_p, pos_p, None, wts)
    (y_s, gmv_s, dk_s, dv_s, sc_s, sbk_s, sbv_s, ff_s) = _trunk(x_sample, mod_s, pos_s, caches, wts)
    return (y_p, y_s, gmv_s, dk_p, dv_p, dk_s, dv_s, sc_p, sc_s,
            sbk_p, sbv_p, sbk_s, sbv_s, ff_p, ff_s)
```

```python
import functools
import math

import jax
import jax.numpy as jnp
from jax import lax
from jax.experimental import pallas as pl
from jax.experimental.pallas import tpu as pltpu

F32 = jnp.float32
BF16 = jnp.bfloat16

EPS = 1e-6
CHUNK = 64
N_MIXERS = 4
GM_CHUNK = 128
GM_GROUPS = 8
DIFF_HEADS = 8
ROPE_THETA = 500000.0
SB_HEADS = 16

LANES = 128
CARRY_ROWS = 8
NEG_BIG = -1e30
LOG2E = math.log2(math.e)

ROW_TILE = 512
ATTN_TILE = 256
CACHE_TILE = {"diff": 2048, "stick": 512}
CONV_COLS = 256
HEADS_PER_LOOP = 8
VMEM_LIMIT = 56 * 1024 * 1024


def _params(n_axes):
    return pltpu.CompilerParams(dimension_semantics=("arbitrary",) * n_axes,
                                vmem_limit_bytes=VMEM_LIMIT)


def _const_spec(shape):
    zeros = (0,) * len(shape)
    return pl.BlockSpec(shape, lambda *_: zeros, pipeline_mode=pl.Buffered(1))


def _rms(x, g):
    return x * lax.rsqrt(jnp.mean(x * x, axis=-1, keepdims=True) + EPS) * g


def _prenorm(x, g, scale, shift):
    return _rms(x, g) * (1.0 + scale) + shift


def _dot(a, b):
    return jnp.dot(a, b, preferred_element_type=F32)


def _dot_nt(a, b):
    return lax.dot_general(a, b, (((1,), (1,)), ((), ())), preferred_element_type=F32)


def _mod_kernel(c_ref, w_ref, b_ref, o_ref):
    c = c_ref[...]
    cs = (c * jax.nn.sigmoid(c)).astype(BF16)
    o_ref[0] = _dot(cs, w_ref[0].astype(BF16)) + b_ref[0]


def _modulation(c_all, ada_w, ada_b):
    depth, d, _ = ada_w.shape
    rows = c_all.shape[0]
    out = pl.pallas_call(
        _mod_kernel,
        grid=(depth, 6),
        in_specs=[
            pl.BlockSpec((rows, d), lambda i, j: (0, 0)),
            pl.BlockSpec((1, d, d), lambda i, j: (i, 0, j)),
            pl.BlockSpec((1, 1, d), lambda i, j: (i * 6 + j, 0, 0)),
        ],
        out_specs=pl.BlockSpec((1, rows, d), lambda i, j: (i * 6 + j, 0, 0)),
        out_shape=jax.ShapeDtypeStruct((depth * 6, rows, d), F32),
        compiler_params=_params(2),
    )(c_all, ada_w, ada_b.reshape(depth * 6, 1, d))
    return out.reshape(depth, 6, rows, d)


def _conv3(val, cw, hist, lo, width, tm):
    cols = slice(lo, lo + width)
    hist[CARRY_ROWS:CARRY_ROWS + tm, cols] = val
    back1 = hist[CARRY_ROWS - 1:CARRY_ROWS - 1 + tm, cols]
    back2 = hist[CARRY_ROWS - 2:CARRY_ROWS - 2 + tm, cols]
    hist[0:CARRY_ROWS, cols] = hist[tm:tm + CARRY_ROWS, cols]
    return cw[2:3] * val + cw[1:2] * back1 + cw[0:1] * back2


def _init_hist(hist, prev_ref):
    @pl.when(pl.program_id(1) == 0)
    def _():
        hist[0:CARRY_ROWS, :] = jnp.zeros((CARRY_ROWS, hist.shape[1]), F32)
        hist[CARRY_ROWS - 2:CARRY_ROWS, :] = prev_ref[0]


def _ffn_kernel(x_ref, mod_ref, gpre_ref, gpost_ref, prev_ref, win_ref, cw_ref, cb_ref, wout_ref,
                y_ref, st_ref, hist, act, *, tm, dff, fc):
    _init_hist(hist, prev_ref)
    x = x_ref[0]
    mod = mod_ref[0]
    h = _prenorm(x, gpre_ref[...], mod[4:5], mod[3:4]).astype(BF16)
    for c in range(dff // fc):
        lo = c * fc
        g = _dot(h, win_ref[:, lo:lo + fc])
        u = _dot(h, win_ref[:, dff + lo:dff + lo + fc])
        y = _conv3(g, cw_ref[:, lo:lo + fc], hist, lo, fc, tm) + cb_ref[:, lo:lo + fc]
        act[:, lo:lo + fc] = (y * jax.nn.sigmoid(y) * u).astype(BF16)
    st_ref[0] = hist[CARRY_ROWS - 2:CARRY_ROWS, :]
    y_ref[0] = x + mod[5:6] * _rms(_dot(act[...], wout_ref[...]), gpost_ref[...])


def _ffn(x, mod, gpre, gpost, prev, w_in, conv_w, conv_b, w_out):
    b, s, d = x.shape
    dff = w_out.shape[0]
    tm = min(ROW_TILE, s)
    fc = CONV_COLS
    kern = functools.partial(_ffn_kernel, tm=tm, dff=dff, fc=fc)
    return pl.pallas_call(
        kern,
        grid=(b, s // tm),
        in_specs=[
            pl.BlockSpec((1, tm, d), lambda i, j: (i, j, 0)),
            pl.BlockSpec((1, 6, d), lambda i, j: (i, 0, 0)),
            _const_spec((1, d)),
            _const_spec((1, d)),
            pl.BlockSpec((1, 2, dff), lambda i, j: (i, 0, 0)),
            _const_spec((d, 2 * dff)),
            _const_spec((3, dff)),
            _const_spec((1, dff)),
            _const_spec((dff, d)),
        ],
        out_specs=[
            pl.BlockSpec((1, tm, d), lambda i, j: (i, j, 0)),
            pl.BlockSpec((1, 2, dff), lambda i, j: (i, 0, 0)),
        ],
        out_shape=[
            jax.ShapeDtypeStruct((b, s, d), F32),
            jax.ShapeDtypeStruct((b, 2, dff), F32),
        ],
        scratch_shapes=[pltpu.VMEM((CARRY_ROWS + tm, dff), F32), pltpu.VMEM((tm, dff), BF16)],
        compiler_params=_params(2),
    )(x, mod, gpre, gpost, prev, w_in, conv_w, conv_b.reshape(1, dff), w_out)


def _sconv_kernel(x_ref, mod_ref, gpre_ref, gpost_ref, prev_ref, win_ref, cw_ref, wout_ref,
                  y_ref, st_ref, hist, act, *, tm, dsc, fc):
    _init_hist(hist, prev_ref)
    x = x_ref[0]
    mod = mod_ref[0]
    h = _prenorm(x, gpre_ref[...], mod[1:2], mod[0:1]).astype(BF16)
    for c in range(dsc // fc):
        lo = c * fc
        b_gate = _dot(h, win_ref[:, lo:lo + fc])
        c_gate = _dot(h, win_ref[:, dsc + lo:dsc + lo + fc])
        xin = _dot(h, win_ref[:, 2 * dsc + lo:2 * dsc + lo + fc])
        y = _conv3(c_gate * xin, cw_ref[:, lo:lo + fc], hist, lo, fc, tm)
        act[:, lo:lo + fc] = (b_gate * y).astype(BF16)
    st_ref[0] = hist[CARRY_ROWS - 2:CARRY_ROWS, :]
    y_ref[0] = x + mod[2:3] * _rms(_dot(act[...], wout_ref[...]), gpost_ref[...])


def _sconv(x, mod, gpre, gpost, prev, w_in, conv_w, w_out):
    b, s, d = x.shape
    dsc = w_out.shape[0]
    tm = min(ROW_TILE, s)
    fc = CONV_COLS
    kern = functools.partial(_sconv_kernel, tm=tm, dsc=dsc, fc=fc)
    return pl.pallas_call(
        kern,
        grid=(b, s // tm),
        in_specs=[
            pl.BlockSpec((1, tm, d), lambda i, j: (i, j, 0)),
            pl.BlockSpec((1, 6, d), lambda i, j: (i, 0, 0)),
            _const_spec((1, d)),
            _const_spec((1, d)),
            pl.BlockSpec((1, 2, dsc), lambda i, j: (i, 0, 0)),
            _const_spec((d, 3 * dsc)),
            _const_spec((3, dsc)),
            _const_spec((dsc, d)),
        ],
        out_specs=[
            pl.BlockSpec((1, tm, d), lambda i, j: (i, j, 0)),
            pl.BlockSpec((1, 2, dsc), lambda i, j: (i, 0, 0)),
        ],
        out_shape=[
            jax.ShapeDtypeStruct((b, s, d), F32),
            jax.ShapeDtypeStruct((b, 2, dsc), F32),
        ],
        scratch_shapes=[pltpu.VMEM((CARRY_ROWS + tm, dsc), F32), pltpu.VMEM((tm, dsc), BF16)],
        compiler_params=_params(2),
    )(x, mod, gpre, gpost, prev, w_in, conv_w, w_out)


def _gmlp_kernel(x_ref, mod_ref, gpre_ref, gpost_ref, win_ref, lng_ref, lnb_ref, ws_ref, bsf_ref,
                 wout_ref, y_ref, v_ref, mixbuf, *, tm, span, dgm):
    x = x_ref[0]
    mod = mod_ref[0]
    h = _prenorm(x, gpre_ref[...], mod[1:2], mod[0:1]).astype(BF16)
    u = jax.nn.gelu(_dot(h, win_ref[:, 0:dgm]))
    v = jax.nn.gelu(_dot(h, win_ref[:, dgm:2 * dgm]))
    vc = v - jnp.mean(v, axis=-1, keepdims=True)
    v = vc * lax.rsqrt(jnp.mean(vc * vc, axis=-1, keepdims=True) + EPS) * lng_ref[...] + lnb_ref[...]
    v_ref[0] = v
    mix_dtype = BF16 if span % 16 == 0 and span >= LANES else F32
    vm = v.astype(mix_dtype)
    gw = dgm // GM_GROUPS
    row = lax.broadcasted_iota(jnp.int32, (span, span), 0)
    col = lax.broadcasted_iota(jnp.int32, (span, span), 1)
    for g in range(GM_GROUPS):
        w = jnp.where(row >= col, ws_ref[g], 0.0).astype(mix_dtype)
        for r in range(tm // span):
            mixbuf[r * span:(r + 1) * span, g * gw:(g + 1) * gw] = (
                _dot(w, vm[r * span:(r + 1) * span, g * gw:(g + 1) * gw]) + bsf_ref[:, g * gw:(g + 1) * gw])
    m = _dot((u * mixbuf[...]).astype(BF16), wout_ref[...])
    y_ref[0] = x + mod[2:3] * _rms(m, gpost_ref[...])


def _gmlp(x, mod, gpre, gpost, w_in, ln_g, ln_b, ws, bs, w_out):
    b, s, d = x.shape
    dgm = w_out.shape[0]
    span = GM_CHUNK if s >= GM_CHUNK else s
    tm = min(ROW_TILE, s)
    ws_l = ws[:, :span, :span]
    bs_tile = jnp.repeat(bs[:, :span].T, dgm // GM_GROUPS, axis=1)
    kern = functools.partial(_gmlp_kernel, tm=tm, span=span, dgm=dgm)
    return pl.pallas_call(
        kern,
        grid=(b, s // tm),
        in_specs=[
            pl.BlockSpec((1, tm, d), lambda i, j: (i, j, 0)),
            pl.BlockSpec((1, 6, d), lambda i, j: (i, 0, 0)),
            _const_spec((1, d)),
            _const_spec((1, d)),
            _const_spec((d, 2 * dgm)),
            _const_spec((1, dgm)),
            _const_spec((1, dgm)),
            _const_spec((GM_GROUPS, span, span)),
            _const_spec((span, dgm)),
            _const_spec((dgm, d)),
        ],
        out_specs=[
            pl.BlockSpec((1, tm, d), lambda i, j: (i, j, 0)),
            pl.BlockSpec((1, tm, dgm), lambda i, j: (i, j, 0)),
        ],
        out_shape=[
            jax.ShapeDtypeStruct((b, s, d), F32),
            jax.ShapeDtypeStruct((b, s, dgm), F32),
        ],
        scratch_shapes=[pltpu.VMEM((tm, dgm), F32)],
        compiler_params=_params(2),
    )(x, mod, gpre, gpost, w_in, ln_g.reshape(1, dgm), ln_b.reshape(1, dgm), ws_l, bs_tile, w_out)


def _rotate(t, cos, sin_fwd, sin_bwd, rot_half):
    outs = []
    for s in range(t.shape[1] // LANES):
        ts = t[:, s * LANES:(s + 1) * LANES]
        ahead = pltpu.roll(ts, LANES - rot_half, 1)
        behind = pltpu.roll(ts, rot_half, 1)
        outs.append(ts * cos + ahead * sin_fwd + behind * sin_bwd)
    return jnp.concatenate(outs, axis=1)


def _qkv_kernel(*refs, d, qscale, rot_half):
    if rot_half:
        (x_ref, mod_ref, gpre_ref, w_ref, cos_ref, sf_ref, sb_ref,
         k_ref, v_ref, qb_ref, kb_ref, vb_ref) = refs
    else:
        x_ref, mod_ref, gpre_ref, w_ref, k_ref, v_ref, qb_ref, kb_ref, vb_ref = refs
    x = x_ref[0]
    mod = mod_ref[0]
    h = _prenorm(x, gpre_ref[...], mod[1:2], mod[0:1]).astype(BF16)
    q = _dot(h, w_ref[:, 0:d])
    k = _dot(h, w_ref[:, d:2 * d])
    v = _dot(h, w_ref[:, 2 * d:3 * d])
    if rot_half:
        cos, sf, sb = cos_ref[...], sf_ref[...], sb_ref[...]
        q = _rotate(q, cos, sf, sb, rot_half)
        k = _rotate(k, cos, sf, sb, rot_half)
    k_ref[0] = k
    v_ref[0] = v
    qb_ref[0] = (q * qscale).astype(BF16)
    kb_ref[0] = k.astype(BF16)
    vb_ref[0] = v.astype(BF16)


def _rope_tables(pos, head_dim, rot_dim):
    half = rot_dim // 2
    inv = ROPE_THETA ** (-jnp.arange(0, rot_dim, 2, dtype=F32) / rot_dim)
    ang = pos.astype(F32)[:, None] * inv[None, :]
    cos, sin = jnp.cos(ang), jnp.sin(ang)
    n = pos.shape[0]
    pad = jnp.zeros((n, head_dim - rot_dim), F32)
    zeros = jnp.zeros((n, half), F32)
    cos_t = jnp.concatenate([cos, cos, pad + 1.0], axis=1)
    sf_t = jnp.concatenate([-sin, zeros, pad], axis=1)
    sb_t = jnp.concatenate([zeros, sin, pad], axis=1)
    rep = LANES // head_dim
    return tuple(jnp.tile(t, (1, rep)) for t in (cos_t, sf_t, sb_t))


def _qkv(x, mod, gpre, w_qkv, head_dim, rope=None):
    b, s, d = x.shape
    tm = min(ROW_TILE, s)
    rot_half = 0
    tables = ()
    table_specs = []
    if rope is not None:
        pos, rot_dim = rope
        rot_half = rot_dim // 2
        tables = _rope_tables(pos, head_dim, rot_dim)
        table_specs = [pl.BlockSpec((tm, LANES), lambda i, j: (j, 0))] * 3
    kern = functools.partial(_qkv_kernel, d=d, qscale=head_dim ** -0.5 * LOG2E, rot_half=rot_half)
    tile = pl.BlockSpec((1, tm, d), lambda i, j: (i, j, 0))
    return pl.pallas_call(
        kern,
        grid=(b, s // tm),
        in_specs=[tile, pl.BlockSpec((1, 6, d), lambda i, j: (i, 0, 0)), _const_spec((1, d)),
                  _const_spec((d, 3 * d))] + table_specs,
        out_specs=[tile] * 5,
        out_shape=[jax.ShapeDtypeStruct((b, s, d), F32)] * 2 + [jax.ShapeDtypeStruct((b, s, d), BF16)] * 3,
        compiler_params=_params(2),
    )(x, mod, gpre, w_qkv, *tables)


def _stack_halves(qh):
    lane = lax.broadcasted_iota(jnp.int32, qh.shape, 1)
    zero = jnp.zeros_like(qh)
    return jnp.concatenate([jnp.where(lane < LANES // 2, qh, zero),
                            jnp.where(lane >= LANES // 2, qh, zero)], axis=0)


def _lane_tiles(x):
    return [x[:, t * LANES:(t + 1) * LANES] for t in range(x.shape[1] // LANES)]


def _row_bcast(col):
    return jnp.broadcast_to(col, (col.shape[0], LANES))


def _softmax_block(q2, ks, vs, state, h, mask=None, first=False):
    m_ref, l_ref, acc_ref = state
    s = _dot_nt(q2, ks)
    if mask is not None:
        s = jnp.where(mask, s, NEG_BIG)
    tiles = _lane_tiles(s)
    m_blk = _row_bcast(jnp.max(functools.reduce(jnp.maximum, tiles), axis=-1, keepdims=True))
    m_new = m_blk if first else jnp.maximum(m_ref[h], m_blk)
    ps = [jnp.exp2(t - m_new) for t in tiles]
    p_sum = functools.reduce(jnp.add, ps)
    pv = _dot(jnp.concatenate(ps, axis=1).astype(BF16), vs)
    if first:
        l_ref[h] = p_sum
        acc_ref[h] = pv
    else:
        alpha = jnp.exp2(m_ref[h] - m_new)
        l_ref[h] = alpha * l_ref[h] + p_sum
        acc_ref[h] = alpha * acc_ref[h] + pv
    m_ref[h] = m_new


def _diff_lambda(lam_ref, lam_init):
    lp = lam_ref[...]
    s1 = jnp.sum(lp[0:1] * lp[1:2], axis=-1, keepdims=True)
    s2 = jnp.sum(lp[2:3] * lp[3:4], axis=-1, keepdims=True)
    return jnp.exp(s1) - jnp.exp(s2) + lam_init


def _diff_head_out(state, h, rows, lam, sg, lam_init):
    _, l_ref, acc_ref = state
    l, acc = jnp.sum(l_ref[h], axis=-1, keepdims=True), acc_ref[h]
    o = acc[:rows] / l[:rows] - lam * (acc[rows:] / l[rows:])
    return (_rms(o, sg) * (1.0 - lam_init)).astype(BF16)


def _suffix_ones(n):
    row = lax.broadcasted_iota(jnp.int32, (n, n), 0)
    col = lax.broadcasted_iota(jnp.int32, (n, n), 1)
    return jnp.where(row > col, 1.0, 0.0).astype(BF16)


def _stick_block(q2, ks, vs, state, h, ones_u, mask=None, first=False):
    run_ref, acc_ref = state
    z = _dot_nt(q2, ks)
    neg_abs = pltpu.bitcast(pltpu.bitcast(z, jnp.uint32) | jnp.uint32(0x80000000), F32)
    log_beta = jnp.minimum(z, 0.0) - jnp.log2(1.0 + jnp.exp2(neg_abs))
    log_keep = log_beta - z
    if mask is not None:
        log_keep = jnp.where(mask, log_keep, 0.0)
    expo = log_beta + _dot(log_keep.astype(BF16), ones_u)
    if not first:
        run = run_ref[h]
        expo = expo + jnp.concatenate([run] * (z.shape[1] // LANES), axis=1)
    w = jnp.exp2(expo)
    if mask is not None:
        w = jnp.where(mask, w, 0.0)
    pv = _dot(w.astype(BF16), vs)
    keep_sum = functools.reduce(jnp.add, _lane_tiles(log_keep))
    keep_sum = _row_bcast(jnp.sum(keep_sum, axis=-1, keepdims=True))
    if first:
        acc_ref[h] = pv
        run_ref[h] = keep_sum
    else:
        acc_ref[h] = acc_ref[h] + pv
        run_ref[h] = run + keep_sum


def _stick_head_out(state, h, rows):
    acc = state[1][h]
    lane = lax.broadcasted_iota(jnp.int32, (rows, LANES), 1)
    return jnp.where(lane < LANES // 2, acc[:rows], acc[rows:]).astype(BF16)


def _attn_state(kind, n_slabs, rows):
    n = 3 if kind == "diff" else 2
    return [pltpu.VMEM((n_slabs, rows, LANES), F32)] * n


def _rel_index(tq, tk):
    qi = lax.broadcasted_iota(jnp.int32, (2 * tq, tk), 0)
    qi = jnp.where(qi >= tq, qi - tq, qi)
    ki = lax.broadcasted_iota(jnp.int32, (2 * tq, tk), 1)
    return qi, ki


def _attn_prompt_kernel(*refs, kind, tq, d, lam_init):
    if kind == "diff":
        x_ref, mod_ref, gpost_ref, q_ref, k_ref, v_ref, wout_ref, lam_ref, sg_ref, y_ref, obuf, *state = refs
        lam = _diff_lambda(lam_ref, lam_init)
        sg = sg_ref[...]
    else:
        x_ref, mod_ref, gpost_ref, q_ref, k_ref, v_ref, wout_ref, y_ref, obuf, *state = refs
        ones_u = _suffix_ones(tq)
    j = pl.program_id(1)
    qi, ki = _rel_index(tq, tq)
    if kind == "diff":
        diag_mask = (ki // CHUNK) <= (qi // CHUNK)
    else:
        diag_mask = ki < qi

    def kv(kb, sl):
        start = pl.multiple_of(kb * tq, tq)
        return k_ref[0, pl.ds(start, tq), sl], v_ref[0, pl.ds(start, tq), sl]

    n_slabs = d // LANES
    for h0 in range(0, n_slabs, HEADS_PER_LOOP):
        heads = range(h0, min(h0 + HEADS_PER_LOOP, n_slabs))
        slabs = {h: slice(h * LANES, (h + 1) * LANES) for h in heads}
        q2s = {h: _stack_halves(q_ref[0, :, slabs[h]]) for h in heads}
        if kind == "diff":
            for h in heads:
                _softmax_block(q2s[h], *kv(j, slabs[h]), state, h, diag_mask, first=True)

            @pl.loop(0, j)
            def _(kb):
                for h in heads:
                    _softmax_block(q2s[h], *kv(kb, slabs[h]), state, h)

            for h in heads:
                obuf[:, slabs[h]] = _diff_head_out(state, h, tq, lam, sg, lam_init)
        else:
            for h in heads:
                _stick_block(q2s[h], *kv(j, slabs[h]), state, h, ones_u, diag_mask, first=True)

            @pl.loop(0, j)
            def _(i):
                for h in heads:
                    _stick_block(q2s[h], *kv(j - 1 - i, slabs[h]), state, h, ones_u)

            for h in heads:
                obuf[:, slabs[h]] = _stick_head_out(state, h, tq)

    m = _dot(obuf[...], wout_ref[...])
    y_ref[0] = x_ref[0] + mod_ref[0][2:3] * _rms(m, gpost_ref[...])


def _attn_prompt(kind, x, mod, gpost, qb, kb, vb, w_out, extra=(), lam_init=0.0):
    b, s, d = x.shape
    tq = min(ATTN_TILE, s)
    kern = functools.partial(_attn_prompt_kernel, kind=kind, tq=tq, d=d, lam_init=lam_init)
    tile = pl.BlockSpec((1, tq, d), lambda i, j: (i, j, 0))
    whole = pl.BlockSpec((1, s, d), lambda i, j: (i, 0, 0))
    return pl.pallas_call(
        kern,
        grid=(b, s // tq),
        in_specs=[tile, pl.BlockSpec((1, 6, d), lambda i, j: (i, 0, 0)), _const_spec((1, d)),
                  tile, whole, whole, _const_spec((d, d))] + [_const_spec(e.shape) for e in extra],
        out_specs=tile,
        out_shape=jax.ShapeDtypeStruct((b, s, d), F32),
        scratch_shapes=[pltpu.VMEM((tq, d), BF16)] + _attn_state(kind, d // LANES, 2 * tq),
        compiler_params=_params(2),
    )(x, mod, gpost, qb, kb, vb, w_out, *extra)


def _attn_stream_kernel(*refs, kind, s_new, past, tkc, d, lam_init):
    if kind == "diff":
        (x_ref, mod_ref, gpost_ref, q_ref, kc_ref, vc_ref, kn_ref, vn_ref, wout_ref, lam_ref, sg_ref,
         y_ref, obuf, *state) = refs
        lam = _diff_lambda(lam_ref, lam_init)
        sg = sg_ref[...]
    else:
        x_ref, mod_ref, gpost_ref, q_ref, kc_ref, vc_ref, kn_ref, vn_ref, wout_ref, y_ref, obuf, *state = refs
        ones_new = _suffix_ones(kn_ref.shape[1])
        ones_cache = _suffix_ones(tkc)
    qi, ki = _rel_index(s_new, kn_ref.shape[1])
    if kind == "diff":
        new_mask = (((past + ki) // CHUNK) <= ((past + qi) // CHUNK)) & (ki < s_new)
    else:
        new_mask = (ki < qi) & (ki < s_new)
    n_cache = past // tkc

    for h in range(d // LANES):
        sl = slice(h * LANES, (h + 1) * LANES)
        q2 = _stack_halves(q_ref[0, :, sl])

        def cache_kv(kb):
            rows = slice(kb * tkc, (kb + 1) * tkc)
            return kc_ref[0, rows, sl].astype(BF16), vc_ref[0, rows, sl].astype(BF16)

        if kind == "diff":
            _softmax_block(q2, kn_ref[0, :, sl], vn_ref[0, :, sl], state, h, new_mask, first=True)
            for kb in range(n_cache):
                _softmax_block(q2, *cache_kv(kb), state, h)
            obuf[:, sl] = _diff_head_out(state, h, s_new, lam, sg, lam_init)
        else:
            _stick_block(q2, kn_ref[0, :, sl], vn_ref[0, :, sl], state, h, ones_new, new_mask, first=True)
            for kb in reversed(range(n_cache)):
                _stick_block(q2, *cache_kv(kb), state, h, ones_cache)
            obuf[:, sl] = _stick_head_out(state, h, s_new)

    m = _dot(obuf[...], wout_ref[...])
    y_ref[0] = x_ref[0] + mod_ref[0][2:3] * _rms(m, gpost_ref[...])


def _attn_stream(kind, x, mod, gpost, qb, k_cache, v_cache, kb_new, vb_new, w_out, extra=(), lam_init=0.0):
    b, s, d = x.shape
    past = k_cache.shape[1]
    tkc = min(CACHE_TILE[kind], past)
    pad = (-s) % LANES
    kn = jnp.pad(kb_new, ((0, 0), (0, pad), (0, 0)))
    vn = jnp.pad(vb_new, ((0, 0), (0, pad), (0, 0)))
    kern = functools.partial(_attn_stream_kernel, kind=kind, s_new=s, past=past, tkc=tkc, d=d, lam_init=lam_init)
    tile = pl.BlockSpec((1, s, d), lambda i: (i, 0, 0))
    cache = pl.BlockSpec((1, past, d), lambda i: (i, 0, 0))
    new = pl.BlockSpec((1, s + pad, d), lambda i: (i, 0, 0))
    return pl.pallas_call(
        kern,
        grid=(b,),
        in_specs=[tile, pl.BlockSpec((1, 6, d), lambda i: (i, 0, 0)), _const_spec((1, d)),
                  tile, cache, cache, new, new, _const_spec((d, d))] + [_const_spec(e.shape) for e in extra],
        out_specs=tile,
        out_shape=jax.ShapeDtypeStruct((b, s, d), F32),
        scratch_shapes=[pltpu.VMEM((s, d), BF16)] + _attn_state(kind, d // LANES, 2 * s),
        compiler_params=_params(1),
    )(x, mod, gpost, qb, k_cache, v_cache, kn, vn, w_out, *extra)


def _trunk(x, mod, pos, caches, wts):
    (norm_g, gm_w_in, gm_ln_g, gm_ln_b, gm_ws, gm_bs, gm_w_out,
     diff_w_qkv, diff_lambda, diff_subln_g, diff_w_out,
     sc_w_in, sc_conv_w, sc_w_out, sb_w_qkv, sb_w_out,
     ffn_w_in, ffn_conv_w, ffn_conv_b, ffn_w_out) = wts
    b, s, d = x.shape
    depth = norm_g.shape[0]
    dff = ffn_w_out.shape[1]
    gm_v, dk, dv, scs, sbk, sbv, ffs = [], [], [], [], [], [], []
    for i in range(depth):
        kind, j = i % N_MIXERS, i // N_MIXERS
        m_i = mod[i]
        g = [norm_g[i, n].reshape(1, d) for n in range(4)]
        if kind == 0:
            x, v_rows = _gmlp(x, m_i, g[0], g[1], gm_w_in[j], gm_ln_g[j], gm_ln_b[j], gm_ws[j], gm_bs[j],
                              gm_w_out[j])
            gm_v.append(v_rows)
        elif kind == 1:
            hd = d // (2 * DIFF_HEADS)
            lam_init = 0.8 - 0.6 * math.exp(-0.3 * i)
            k, v, qb, kb, vb = _qkv(x, m_i, g[0], diff_w_qkv[j], hd, rope=(pos, hd // 4))
            extra = (diff_lambda[j], diff_subln_g[j].reshape(1, 2 * hd))
            if caches is None:
                x = _attn_prompt("diff", x, m_i, g[1], qb, kb, vb, diff_w_out[j], extra, lam_init)
            else:
                kc = caches["diff_k"][j].reshape(b, -1, d)
                vc = caches["diff_v"][j].reshape(b, -1, d)
                x = _attn_stream("diff", x, m_i, g[1], qb, kc, vc, kb, vb, diff_w_out[j], extra, lam_init)
            dk.append(k.reshape(b, s, DIFF_HEADS, 2, hd))
            dv.append(v.reshape(b, s, DIFF_HEADS, 2 * hd))
        elif kind == 2:
            prev = jnp.zeros((b, 2, d), F32) if caches is None else caches["sconv"][j]
            x, st = _sconv(x, m_i, g[0], g[1], prev, sc_w_in[j], sc_conv_w[j], sc_w_out[j])
            scs.append(st)
        else:
            hd = d // SB_HEADS
            k, v, qb, kb, vb = _qkv(x, m_i, g[0], sb_w_qkv[j], hd)
            if caches is None:
                x = _attn_prompt("stick", x, m_i, g[1], qb, kb, vb, sb_w_out[j])
            else:
                kc = caches["sb_k"][j].reshape(b, -1, d)
                vc = caches["sb_v"][j].reshape(b, -1, d)
                x = _attn_stream("stick", x, m_i, g[1], qb, kc, vc, kb, vb, sb_w_out[j])
            sbk.append(k.reshape(b, s, SB_HEADS, hd))
            sbv.append(v.reshape(b, s, SB_HEADS, hd))
        prev = jnp.zeros((b, 2, dff), F32) if caches is None else caches["ffn"][i]
        x, st = _ffn(x, m_i, g[2], g[3], prev, ffn_w_in[i], ffn_conv_w[i], ffn_conv_b[i], ffn_w_out[i])
        ffs.append(st)
    return (x, jnp.stack(gm_v), jnp.stack(dk), jnp.stack(dv), jnp.stack(scs),
            jnp.stack(sbk), jnp.stack(sbv), jnp.stack(ffs))


def kernel(x_prompt, x_sample, cache_diff_k, cache_diff_v, state_sconv, cache_sb_k, cache_sb_v, state_ffn_conv, c_prompt, c_sample, ada_w, ada_b, norm_g, gm_w_in, gm_ln_g, gm_ln_b, gm_ws, gm_bs, gm_w_out, diff_w_qkv, diff_lambda, diff_subln_g, diff_w_out, sc_w_in, sc_conv_w, sc_w_out, sb_w_qkv, sb_w_out, ffn_w_in, ffn_conv_w, ffn_conv_b, ffn_w_out):
    bp = x_prompt.shape[0]
    past = cache_diff_k.shape[2]
    cast = lambda w: w.astype(BF16)
    wts = (norm_g, cast(gm_w_in), gm_ln_g, gm_ln_b, gm_ws, gm_bs, cast(gm_w_out),
           cast(diff_w_qkv), diff_lambda, diff_subln_g, cast(diff_w_out),
           cast(sc_w_in), sc_conv_w, cast(sc_w_out), cast(sb_w_qkv), cast(sb_w_out),
           cast(ffn_w_in), ffn_conv_w, ffn_conv_b, cast(ffn_w_out))

    mod = _modulation(jnp.concatenate([c_prompt, c_sample], axis=0), ada_w, ada_b)
    mod = jnp.transpose(mod, (0, 2, 1, 3))
    mod_p, mod_s = mod[:, :bp], mod[:, bp:]

    pos_p = jnp.arange(x_prompt.shape[1])
    pos_s = past + jnp.arange(x_sample.shape[1])
    caches = {"diff_k": cache_diff_k, "diff_v": cache_diff_v, "sconv": state_sconv,
              "sb_k": cache_sb_k, "sb_v": cache_sb_v, "ffn": state_ffn_conv}

    (y_p, _, dk_p, dv_p, sc_p, sbk_p, sbv_p, ff_p) = _trunk(x_prompt, mod_p, pos_p, None, wts)
    (y_s, gmv_s, dk_s, dv_s, sc_s, sbk_s, sbv_s, ff_s) = _trunk(x_sample, mod_s, pos_s, caches, wts)
    return (y_p, y_s, gmv_s, dk_p, dv_p, dk_s, dv_s, sc_p, sc_s,
            sbk_p, sbv_p, sbk_s, sbv_s, ff_p, ff_s)
```

```python
import functools
import math

import jax
import jax.numpy as jnp
from jax import lax
from jax.experimental import pallas as pl
from jax.experimental.pallas import tpu as pltpu

F32 = jnp.float32
BF16 = jnp.bfloat16

EPS = 1e-6
CHUNK = 64
N_MIXERS = 4
GM_CHUNK = 128
GM_GROUPS = 8
DIFF_HEADS = 8
ROPE_THETA = 500000.0
SB_HEADS = 16

LANES = 128
CARRY_ROWS = 8
NEG_BIG = -1e30
LOG2E = math.log2(math.e)
STICK_FLOOR = -160.0

ROW_TILE = 512
ATTN_TILE = 256
CACHE_TILE = {"diff": 2048, "stick": 512}
CONV_COLS = 256
HEADS_PER_LOOP = 8
VMEM_LIMIT = 56 * 1024 * 1024


def _params(n_axes):
    return pltpu.CompilerParams(dimension_semantics=("arbitrary",) * n_axes,
                                vmem_limit_bytes=VMEM_LIMIT)


def _const_spec(shape):
    zeros = (0,) * len(shape)
    return pl.BlockSpec(shape, lambda *_: zeros, pipeline_mode=pl.Buffered(1))


def _rms(x, g):
    return x * lax.rsqrt(jnp.mean(x * x, axis=-1, keepdims=True) + EPS) * g


def _prenorm(x, g, scale, shift):
    return _rms(x, g) * (1.0 + scale) + shift


def _dot(a, b):
    return jnp.dot(a, b, preferred_element_type=F32)


def _dot_nt(a, b):
    return lax.dot_general(a, b, (((1,), (1,)), ((), ())), preferred_element_type=F32)


def _mod_kernel(c_ref, w_ref, b_ref, o_ref):
    c = c_ref[...]
    cs = (c * jax.nn.sigmoid(c)).astype(BF16)
    o_ref[0] = _dot(cs, w_ref[0].astype(BF16)) + b_ref[0]


def _modulation(c_all, ada_w, ada_b):
    depth, d, _ = ada_w.shape
    rows = c_all.shape[0]
    out = pl.pallas_call(
        _mod_kernel,
        grid=(depth, 6),
        in_specs=[
            pl.BlockSpec((rows, d), lambda i, j: (0, 0)),
            pl.BlockSpec((1, d, d), lambda i, j: (i, 0, j)),
            pl.BlockSpec((1, 1, d), lambda i, j: (i * 6 + j, 0, 0)),
        ],
        out_specs=pl.BlockSpec((1, rows, d), lambda i, j: (i * 6 + j, 0, 0)),
        out_shape=jax.ShapeDtypeStruct((depth * 6, rows, d), F32),
        compiler_params=_params(2),
    )(c_all, ada_w, ada_b.reshape(depth * 6, 1, d))
    return out.reshape(depth, 6, rows, d)


def _conv3(val, cw, hist, lo, width, tm):
    cols = slice(lo, lo + width)
    hist[CARRY_ROWS:CARRY_ROWS + tm, cols] = val
    back1 = hist[CARRY_ROWS - 1:CARRY_ROWS - 1 + tm, cols]
    back2 = hist[CARRY_ROWS - 2:CARRY_ROWS - 2 + tm, cols]
    hist[0:CARRY_ROWS, cols] = hist[tm:tm + CARRY_ROWS, cols]
    return cw[2:3] * val + cw[1:2] * back1 + cw[0:1] * back2


def _init_hist(hist, prev_ref):
    @pl.when(pl.program_id(1) == 0)
    def _():
        hist[0:CARRY_ROWS, :] = jnp.zeros((CARRY_ROWS, hist.shape[1]), F32)
        hist[CARRY_ROWS - 2:CARRY_ROWS, :] = prev_ref[0]


def _ffn_kernel(x_ref, mod_ref, gpre_ref, gpost_ref, prev_ref, win_ref, cw_ref, cb_ref, wout_ref,
                y_ref, st_ref, hist, act, *, tm, dff, fc):
    _init_hist(hist, prev_ref)
    x = x_ref[0]
    mod = mod_ref[0]
    h = _prenorm(x, gpre_ref[...], mod[4:5], mod[3:4]).astype(BF16)
    for c in range(dff // fc):
        lo = c * fc
        g = _dot(h, win_ref[:, lo:lo + fc])
        u = _dot(h, win_ref[:, dff + lo:dff + lo + fc])
        y = _conv3(g, cw_ref[:, lo:lo + fc], hist, lo, fc, tm) + cb_ref[:, lo:lo + fc]
        act[:, lo:lo + fc] = (y * jax.nn.sigmoid(y) * u).astype(BF16)
    st_ref[0] = hist[CARRY_ROWS - 2:CARRY_ROWS, :]
    y_ref[0] = x + mod[5:6] * _rms(_dot(act[...], wout_ref[...]), gpost_ref[...])


def _ffn(x, mod, gpre, gpost, prev, w_in, conv_w, conv_b, w_out):
    b, s, d = x.shape
    dff = w_out.shape[0]
    tm = min(ROW_TILE, s)
    fc = CONV_COLS
    kern = functools.partial(_ffn_kernel, tm=tm, dff=dff, fc=fc)
    return pl.pallas_call(
        kern,
        grid=(b, s // tm),
        in_specs=[
            pl.BlockSpec((1, tm, d), lambda i, j: (i, j, 0)),
            pl.BlockSpec((1, 6, d), lambda i, j: (i, 0, 0)),
            _const_spec((1, d)),
            _const_spec((1, d)),
            pl.BlockSpec((1, 2, dff), lambda i, j: (i, 0, 0)),
            _const_spec((d, 2 * dff)),
            _const_spec((3, dff)),
            _const_spec((1, dff)),
            _const_spec((dff, d)),
        ],
        out_specs=[
            pl.BlockSpec((1, tm, d), lambda i, j: (i, j, 0)),
            pl.BlockSpec((1, 2, dff), lambda i, j: (i, 0, 0)),
        ],
        out_shape=[
            jax.ShapeDtypeStruct((b, s, d), F32),
            jax.ShapeDtypeStruct((b, 2, dff), F32),
        ],
        scratch_shapes=[pltpu.VMEM((CARRY_ROWS + tm, dff), F32), pltpu.VMEM((tm, dff), BF16)],
        compiler_params=_params(2),
    )(x, mod, gpre, gpost, prev, w_in, conv_w, conv_b.reshape(1, dff), w_out)


def _sconv_kernel(x_ref, mod_ref, gpre_ref, gpost_ref, prev_ref, win_ref, cw_ref, wout_ref,
                  y_ref, st_ref, hist, act, *, tm, dsc, fc):
    _init_hist(hist, prev_ref)
    x = x_ref[0]
    mod = mod_ref[0]
    h = _prenorm(x, gpre_ref[...], mod[1:2], mod[0:1]).astype(BF16)
    for c in range(dsc // fc):
        lo = c * fc
        b_gate = _dot(h, win_ref[:, lo:lo + fc])
        c_gate = _dot(h, win_ref[:, dsc + lo:dsc + lo + fc])
        xin = _dot(h, win_ref[:, 2 * dsc + lo:2 * dsc + lo + fc])
        y = _conv3(c_gate * xin, cw_ref[:, lo:lo + fc], hist, lo, fc, tm)
        act[:, lo:lo + fc] = (b_gate * y).astype(BF16)
    st_ref[0] = hist[CARRY_ROWS - 2:CARRY_ROWS, :]
    y_ref[0] = x + mod[2:3] * _rms(_dot(act[...], wout_ref[...]), gpost_ref[...])


def _sconv(x, mod, gpre, gpost, prev, w_in, conv_w, w_out):
    b, s, d = x.shape
    dsc = w_out.shape[0]
    tm = min(ROW_TILE, s)
    fc = CONV_COLS
    kern = functools.partial(_sconv_kernel, tm=tm, dsc=dsc, fc=fc)
    return pl.pallas_call(
        kern,
        grid=(b, s // tm),
        in_specs=[
            pl.BlockSpec((1, tm, d), lambda i, j: (i, j, 0)),
            pl.BlockSpec((1, 6, d), lambda i, j: (i, 0, 0)),
            _const_spec((1, d)),
            _const_spec((1, d)),
            pl.BlockSpec((1, 2, dsc), lambda i, j: (i, 0, 0)),
            _const_spec((d, 3 * dsc)),
            _const_spec((3, dsc)),
            _const_spec((dsc, d)),
        ],
        out_specs=[
            pl.BlockSpec((1, tm, d), lambda i, j: (i, j, 0)),
            pl.BlockSpec((1, 2, dsc), lambda i, j: (i, 0, 0)),
        ],
        out_shape=[
            jax.ShapeDtypeStruct((b, s, d), F32),
            jax.ShapeDtypeStruct((b, 2, dsc), F32),
        ],
        scratch_shapes=[pltpu.VMEM((CARRY_ROWS + tm, dsc), F32), pltpu.VMEM((tm, dsc), BF16)],
        compiler_params=_params(2),
    )(x, mod, gpre, gpost, prev, w_in, conv_w, w_out)


def _gmlp_kernel(x_ref, mod_ref, gpre_ref, gpost_ref, win_ref, lng_ref, lnb_ref, ws_ref, bsf_ref,
                 wout_ref, y_ref, v_ref, mixbuf, *, tm, span, dgm):
    x = x_ref[0]
    mod = mod_ref[0]
    h = _prenorm(x, gpre_ref[...], mod[1:2], mod[0:1]).astype(BF16)
    u = jax.nn.gelu(_dot(h, win_ref[:, 0:dgm]))
    v = jax.nn.gelu(_dot(h, win_ref[:, dgm:2 * dgm]))
    vc = v - jnp.mean(v, axis=-1, keepdims=True)
    v = vc * lax.rsqrt(jnp.mean(vc * vc, axis=-1, keepdims=True) + EPS) * lng_ref[...] + lnb_ref[...]
    v_ref[0] = v
    mix_dtype = BF16 if span % 16 == 0 and span >= LANES else F32
    vm = v.astype(mix_dtype)
    gw = dgm // GM_GROUPS
    row = lax.broadcasted_iota(jnp.int32, (span, span), 0)
    col = lax.broadcasted_iota(jnp.int32, (span, span), 1)
    for g in range(GM_GROUPS):
        w = jnp.where(row >= col, ws_ref[g], 0.0).astype(mix_dtype)
        for r in range(tm // span):
            mixbuf[r * span:(r + 1) * span, g * gw:(g + 1) * gw] = (
                _dot(w, vm[r * span:(r + 1) * span, g * gw:(g + 1) * gw]) + bsf_ref[:, g * gw:(g + 1) * gw])
    m = _dot((u * mixbuf[...]).astype(BF16), wout_ref[...])
    y_ref[0] = x + mod[2:3] * _rms(m, gpost_ref[...])


def _gmlp(x, mod, gpre, gpost, w_in, ln_g, ln_b, ws, bs, w_out):
    b, s, d = x.shape
    dgm = w_out.shape[0]
    span = GM_CHUNK if s >= GM_CHUNK else s
    tm = min(ROW_TILE, s)
    ws_l = ws[:, :span, :span]
    bs_tile = jnp.repeat(bs[:, :span].T, dgm // GM_GROUPS, axis=1)
    kern = functools.partial(_gmlp_kernel, tm=tm, span=span, dgm=dgm)
    return pl.pallas_call(
        kern,
        grid=(b, s // tm),
        in_specs=[
            pl.BlockSpec((1, tm, d), lambda i, j: (i, j, 0)),
            pl.BlockSpec((1, 6, d), lambda i, j: (i, 0, 0)),
            _const_spec((1, d)),
            _const_spec((1, d)),
            _const_spec((d, 2 * dgm)),
            _const_spec((1, dgm)),
            _const_spec((1, dgm)),
            _const_spec((GM_GROUPS, span, span)),
            _const_spec((span, dgm)),
            _const_spec((dgm, d)),
        ],
        out_specs=[
            pl.BlockSpec((1, tm, d), lambda i, j: (i, j, 0)),
            pl.BlockSpec((1, tm, dgm), lambda i, j: (i, j, 0)),
        ],
        out_shape=[
            jax.ShapeDtypeStruct((b, s, d), F32),
            jax.ShapeDtypeStruct((b, s, dgm), F32),
        ],
        scratch_shapes=[pltpu.VMEM((tm, dgm), F32)],
        compiler_params=_params(2),
    )(x, mod, gpre, gpost, w_in, ln_g.reshape(1, dgm), ln_b.reshape(1, dgm), ws_l, bs_tile, w_out)


def _rotate(t, cos, sin_fwd, sin_bwd, rot_half):
    outs = []
    for s in range(t.shape[1] // LANES):
        ts = t[:, s * LANES:(s + 1) * LANES]
        ahead = pltpu.roll(ts, LANES - rot_half, 1)
        behind = pltpu.roll(ts, rot_half, 1)
        outs.append(ts * cos + ahead * sin_fwd + behind * sin_bwd)
    return jnp.concatenate(outs, axis=1)


def _qkv_kernel(*refs, d, qscale, rot_half):
    if rot_half:
        (x_ref, mod_ref, gpre_ref, w_ref, cos_ref, sf_ref, sb_ref,
         k_ref, v_ref, qb_ref, kb_ref, vb_ref) = refs
    else:
        x_ref, mod_ref, gpre_ref, w_ref, k_ref, v_ref, qb_ref, kb_ref, vb_ref = refs
    x = x_ref[0]
    mod = mod_ref[0]
    h = _prenorm(x, gpre_ref[...], mod[1:2], mod[0:1]).astype(BF16)
    q = _dot(h, w_ref[:, 0:d])
    k = _dot(h, w_ref[:, d:2 * d])
    v = _dot(h, w_ref[:, 2 * d:3 * d])
    if rot_half:
        cos, sf, sb = cos_ref[...], sf_ref[...], sb_ref[...]
        q = _rotate(q, cos, sf, sb, rot_half)
        k = _rotate(k, cos, sf, sb, rot_half)
    k_ref[0] = k
    v_ref[0] = v
    qb_ref[0] = (q * qscale).astype(BF16)
    kb_ref[0] = k.astype(BF16)
    vb_ref[0] = v.astype(BF16)


def _rope_tables(pos, head_dim, rot_dim):
    half = rot_dim // 2
    inv = ROPE_THETA ** (-jnp.arange(0, rot_dim, 2, dtype=F32) / rot_dim)
    ang = pos.astype(F32)[:, None] * inv[None, :]
    cos, sin = jnp.cos(ang), jnp.sin(ang)
    n = pos.shape[0]
    pad = jnp.zeros((n, head_dim - rot_dim), F32)
    zeros = jnp.zeros((n, half), F32)
    cos_t = jnp.concatenate([cos, cos, pad + 1.0], axis=1)
    sf_t = jnp.concatenate([-sin, zeros, pad], axis=1)
    sb_t = jnp.concatenate([zeros, sin, pad], axis=1)
    rep = LANES // head_dim
    return tuple(jnp.tile(t, (1, rep)) for t in (cos_t, sf_t, sb_t))


def _qkv(x, mod, gpre, w_qkv, head_dim, rope=None):
    b, s, d = x.shape
    tm = min(ROW_TILE, s)
    rot_half = 0
    tables = ()
    table_specs = []
    if rope is not None:
        pos, rot_dim = rope
        rot_half = rot_dim // 2
        tables = _rope_tables(pos, head_dim, rot_dim)
        table_specs = [pl.BlockSpec((tm, LANES), lambda i, j: (j, 0))] * 3
    kern = functools.partial(_qkv_kernel, d=d, qscale=head_dim ** -0.5 * LOG2E, rot_half=rot_half)
    tile = pl.BlockSpec((1, tm, d), lambda i, j: (i, j, 0))
    return pl.pallas_call(
        kern,
        grid=(b, s // tm),
        in_specs=[tile, pl.BlockSpec((1, 6, d), lambda i, j: (i, 0, 0)), _const_spec((1, d)),
                  _const_spec((d, 3 * d))] + table_specs,
        out_specs=[tile] * 5,
        out_shape=[jax.ShapeDtypeStruct((b, s, d), F32)] * 2 + [jax.ShapeDtypeStruct((b, s, d), BF16)] * 3,
        compiler_params=_params(2),
    )(x, mod, gpre, w_qkv, *tables)


def _stack_halves(qh):
    lane = lax.broadcasted_iota(jnp.int32, qh.shape, 1)
    zero = jnp.zeros_like(qh)
    return jnp.concatenate([jnp.where(lane < LANES // 2, qh, zero),
                            jnp.where(lane >= LANES // 2, qh, zero)], axis=0)


def _lane_tiles(x):
    return [x[:, t * LANES:(t + 1) * LANES] for t in range(x.shape[1] // LANES)]


def _row_bcast(col):
    return jnp.broadcast_to(col, (col.shape[0], LANES))


def _softmax_block(q2, ks, vs, state, h, mask=None, first=False):
    m_ref, l_ref, acc_ref = state
    s = _dot_nt(q2, ks)
    if mask is not None:
        s = jnp.where(mask, s, NEG_BIG)
    tiles = _lane_tiles(s)
    m_blk = _row_bcast(jnp.max(functools.reduce(jnp.maximum, tiles), axis=-1, keepdims=True))
    m_new = m_blk if first else jnp.maximum(m_ref[h], m_blk)
    ps = [jnp.exp2(t - m_new) for t in tiles]
    p_sum = functools.reduce(jnp.add, ps)
    pv = _dot(jnp.concatenate(ps, axis=1).astype(BF16), vs)
    if first:
        l_ref[h] = p_sum
        acc_ref[h] = pv
    else:
        alpha = jnp.exp2(m_ref[h] - m_new)
        l_ref[h] = alpha * l_ref[h] + p_sum
        acc_ref[h] = alpha * acc_ref[h] + pv
    m_ref[h] = m_new


def _diff_lambda(lam_ref, lam_init):
    lp = lam_ref[...]
    s1 = jnp.sum(lp[0:1] * lp[1:2], axis=-1, keepdims=True)
    s2 = jnp.sum(lp[2:3] * lp[3:4], axis=-1, keepdims=True)
    return jnp.exp(s1) - jnp.exp(s2) + lam_init


def _diff_head_out(state, h, rows, lam, sg, lam_init):
    _, l_ref, acc_ref = state
    l, acc = jnp.sum(l_ref[h], axis=-1, keepdims=True), acc_ref[h]
    o = acc[:rows] / l[:rows] - lam * (acc[rows:] / l[rows:])
    return (_rms(o, sg) * (1.0 - lam_init)).astype(BF16)


def _suffix_ones(n):
    row = lax.broadcasted_iota(jnp.int32, (n, n), 0)
    col = lax.broadcasted_iota(jnp.int32, (n, n), 1)
    return jnp.where(row > col, 1.0, 0.0).astype(BF16)


def _stick_block(q2, ks, vs, state, h, ones_u, mask=None, first=False):
    run_ref, acc_ref = state
    z = _dot_nt(q2, ks)
    log_beta = jnp.minimum(z, 0.0) - jnp.log2(1.0 + jnp.exp2(-jnp.abs(z)))
    log_keep = log_beta - z
    if mask is not None:
        log_keep = jnp.where(mask, log_keep, 0.0)
    expo = log_beta + _dot(log_keep.astype(BF16), ones_u)
    if not first:
        run = run_ref[h]
        expo = expo + jnp.concatenate([run] * (z.shape[1] // LANES), axis=1)
    w = jnp.exp2(expo)
    if mask is not None:
        w = jnp.where(mask, w, 0.0)
    pv = _dot(w.astype(BF16), vs)
    keep_sum = functools.reduce(jnp.add, _lane_tiles(log_keep))
    keep_sum = _row_bcast(jnp.sum(keep_sum, axis=-1, keepdims=True))
    if first:
        acc_ref[h] = pv
        run_ref[h] = keep_sum
    else:
        acc_ref[h] = acc_ref[h] + pv
        run_ref[h] = run + keep_sum


def _stick_alive(state, heads):
    top = functools.reduce(jnp.maximum, [state[0][h] for h in heads])
    return (jnp.max(top) > STICK_FLOOR).astype(jnp.int32)


def _stick_head_out(state, h, rows):
    acc = state[1][h]
    lane = lax.broadcasted_iota(jnp.int32, (rows, LANES), 1)
    return jnp.where(lane < LANES // 2, acc[:rows], acc[rows:]).astype(BF16)


def _attn_state(kind, n_slabs, rows):
    n = 3 if kind == "diff" else 2
    return [pltpu.VMEM((n_slabs, rows, LANES), F32)] * n


def _rel_index(tq, tk):
    qi = lax.broadcasted_iota(jnp.int32, (2 * tq, tk), 0)
    qi = jnp.where(qi >= tq, qi - tq, qi)
    ki = lax.broadcasted_iota(jnp.int32, (2 * tq, tk), 1)
    return qi, ki


def _attn_prompt_kernel(*refs, kind, tq, d, lam_init):
    if kind == "diff":
        x_ref, mod_ref, gpost_ref, q_ref, k_ref, v_ref, wout_ref, lam_ref, sg_ref, y_ref, obuf, *state = refs
        lam = _diff_lambda(lam_ref, lam_init)
        sg = sg_ref[...]
    else:
        x_ref, mod_ref, gpost_ref, q_ref, k_ref, v_ref, wout_ref, y_ref, obuf, *state = refs
        ones_u = _suffix_ones(tq)
    j = pl.program_id(1)
    qi, ki = _rel_index(tq, tq)
    if kind == "diff":
        diag_mask = (ki // CHUNK) <= (qi // CHUNK)
    else:
        diag_mask = ki < qi

    def kv(kb, sl):
        start = pl.multiple_of(kb * tq, tq)
        return k_ref[0, pl.ds(start, tq), sl], v_ref[0, pl.ds(start, tq), sl]

    n_slabs = d // LANES
    for h0 in range(0, n_slabs, HEADS_PER_LOOP):
        heads = range(h0, min(h0 + HEADS_PER_LOOP, n_slabs))
        slabs = {h: slice(h * LANES, (h + 1) * LANES) for h in heads}
        q2s = {h: _stack_halves(q_ref[0, :, slabs[h]]) for h in heads}
        if kind == "diff":
            for h in heads:
                _softmax_block(q2s[h], *kv(j, slabs[h]), state, h, diag_mask, first=True)

            @pl.loop(0, j)
            def _(kb):
                for h in heads:
                    _softmax_block(q2s[h], *kv(kb, slabs[h]), state, h)

            for h in heads:
                obuf[:, slabs[h]] = _diff_head_out(state, h, tq, lam, sg, lam_init)
        else:
            for h in heads:
                _stick_block(q2s[h], *kv(j, slabs[h]), state, h, ones_u, diag_mask, first=True)

            def older_block(c):
                i, _ = c
                for h in heads:
                    _stick_block(q2s[h], *kv(j - 1 - i, slabs[h]), state, h, ones_u)
                return i + 1, _stick_alive(state, heads)

            lax.while_loop(lambda c: jnp.logical_and(c[0] < j, c[1] > 0), older_block,
                           (jnp.int32(0), _stick_alive(state, heads)))

            for h in heads:
                obuf[:, slabs[h]] = _stick_head_out(state, h, tq)

    m = _dot(obuf[...], wout_ref[...])
    y_ref[0] = x_ref[0] + mod_ref[0][2:3] * _rms(m, gpost_ref[...])


def _attn_prompt(kind, x, mod, gpost, qb, kb, vb, w_out, extra=(), lam_init=0.0):
    b, s, d = x.shape
    tq = min(ATTN_TILE, s)
    kern = functools.partial(_attn_prompt_kernel, kind=kind, tq=tq, d=d, lam_init=lam_init)
    tile = pl.BlockSpec((1, tq, d), lambda i, j: (i, j, 0))
    whole = pl.BlockSpec((1, s, d), lambda i, j: (i, 0, 0))
    return pl.pallas_call(
        kern,
        grid=(b, s // tq),
        in_specs=[tile, pl.BlockSpec((1, 6, d), lambda i, j: (i, 0, 0)), _const_spec((1, d)),
                  tile, whole, whole, _const_spec((d, d))] + [_const_spec(e.shape) for e in extra],
        out_specs=tile,
        out_shape=jax.ShapeDtypeStruct((b, s, d), F32),
        scratch_shapes=[pltpu.VMEM((tq, d), BF16)] + _attn_state(kind, d // LANES, 2 * tq),
        compiler_params=_params(2),
    )(x, mod, gpost, qb, kb, vb, w_out, *extra)


def _attn_stream_kernel(*refs, kind, s_new, past, tkc, d, lam_init, v_rows_by_head):
    if kind == "diff":
        (x_ref, mod_ref, gpost_ref, q_ref, kc_ref, vc_ref, kn_ref, vn_ref, wout_ref, lam_ref, sg_ref,
         y_ref, obuf, *state) = refs
        lam = _diff_lambda(lam_ref, lam_init)
        sg = sg_ref[...]
    else:
        x_ref, mod_ref, gpost_ref, q_ref, kc_ref, vc_ref, kn_ref, vn_ref, wout_ref, y_ref, obuf, *state = refs
        ones_new = _suffix_ones(kn_ref.shape[1])
        ones_cache = _suffix_ones(tkc)
    qi, ki = _rel_index(s_new, kn_ref.shape[1])
    if kind == "diff":
        new_mask = (((past + ki) // CHUNK) <= ((past + qi) // CHUNK)) & (ki < s_new)
    else:
        new_mask = (ki < qi) & (ki < s_new)
    n_cache = past // tkc

    heads = range(d // LANES)
    slabs = {h: slice(h * LANES, (h + 1) * LANES) for h in heads}
    q2s = {h: _stack_halves(q_ref[0, :, slabs[h]]) for h in heads}

    def cache_kv(kb, sl):
        rows = slice(kb * tkc, (kb + 1) * tkc)
        if v_rows_by_head:
            n_heads = d // LANES
            vs = vc_ref[0, pl.ds(kb * tkc * n_heads + sl.start // LANES, tkc, stride=n_heads), :]
        else:
            vs = vc_ref[0, rows, sl]
        return kc_ref[0, rows, sl].astype(BF16), vs.astype(BF16)

    if kind == "diff":
        for h in heads:
            _softmax_block(q2s[h], kn_ref[0, :, slabs[h]], vn_ref[0, :, slabs[h]], state, h, new_mask, first=True)
        for kb in range(n_cache):
            for h in heads:
                _softmax_block(q2s[h], *cache_kv(kb, slabs[h]), state, h)
        for h in heads:
            obuf[:, slabs[h]] = _diff_head_out(state, h, s_new, lam, sg, lam_init)
    else:
        for h in heads:
            _stick_block(q2s[h], kn_ref[0, :, slabs[h]], vn_ref[0, :, slabs[h]], state, h, ones_new, new_mask,
                         first=True)
        for kb in reversed(range(n_cache)):
            @pl.when(_stick_alive(state, heads) > 0)
            def _():
                for h in heads:
                    _stick_block(q2s[h], *cache_kv(kb, slabs[h]), state, h, ones_cache)
        for h in heads:
            obuf[:, slabs[h]] = _stick_head_out(state, h, s_new)

    m = _dot(obuf[...], wout_ref[...])
    y_ref[0] = x_ref[0] + mod_ref[0][2:3] * _rms(m, gpost_ref[...])


def _attn_stream(kind, x, mod, gpost, qb, k_cache, v_cache, kb_new, vb_new, w_out, extra=(), lam_init=0.0):
    b, s, d = x.shape
    v_rows_by_head = v_cache.shape[2] == LANES
    past = k_cache.shape[1]
    tkc = min(CACHE_TILE[kind], past)
    pad = (-s) % LANES
    kn = jnp.pad(kb_new, ((0, 0), (0, pad), (0, 0)))
    vn = jnp.pad(vb_new, ((0, 0), (0, pad), (0, 0)))
    kern = functools.partial(_attn_stream_kernel, kind=kind, s_new=s, past=past, tkc=tkc, d=d, lam_init=lam_init,
                             v_rows_by_head=v_rows_by_head)
    tile = pl.BlockSpec((1, s, d), lambda i: (i, 0, 0))
    cache = pl.BlockSpec((1, past, d), lambda i: (i, 0, 0))
    v_spec = pl.BlockSpec((1,) + v_cache.shape[1:], lambda i: (i, 0, 0))
    new = pl.BlockSpec((1, s + pad, d), lambda i: (i, 0, 0))
    return pl.pallas_call(
        kern,
        grid=(b,),
        in_specs=[tile, pl.BlockSpec((1, 6, d), lambda i: (i, 0, 0)), _const_spec((1, d)),
                  tile, cache, v_spec, new, new, _const_spec((d, d))] + [_const_spec(e.shape) for e in extra],
        out_specs=tile,
        out_shape=jax.ShapeDtypeStruct((b, s, d), F32),
        scratch_shapes=[pltpu.VMEM((s, d), BF16)] + _attn_state(kind, d // LANES, 2 * s),
        compiler_params=_params(1),
    )(x, mod, gpost, qb, k_cache, v_cache, kn, vn, w_out, *extra)


def _trunk(x, mod, pos, caches, wts):
    (norm_g, gm_w_in, gm_ln_g, gm_ln_b, gm_ws, gm_bs, gm_w_out,
     diff_w_qkv, diff_lambda, diff_subln_g, diff_w_out,
     sc_w_in, sc_conv_w, sc_w_out, sb_w_qkv, sb_w_out,
     ffn_w_in, ffn_conv_w, ffn_conv_b, ffn_w_out) = wts
    b, s, d = x.shape
    depth = norm_g.shape[0]
    dff = ffn_w_out.shape[1]
    gm_v, dk, dv, scs, sbk, sbv, ffs = [], [], [], [], [], [], []
    for i in range(depth):
        kind, j = i % N_MIXERS, i // N_MIXERS
        m_i = mod[i]
        g = [norm_g[i, n].reshape(1, d) for n in range(4)]
        if kind == 0:
            x, v_rows = _gmlp(x, m_i, g[0], g[1], gm_w_in[j], gm_ln_g[j], gm_ln_b[j], gm_ws[j], gm_bs[j],
                              gm_w_out[j])
            gm_v.append(v_rows)
        elif kind == 1:
            hd = d // (2 * DIFF_HEADS)
            lam_init = 0.8 - 0.6 * math.exp(-0.3 * i)
            k, v, qb, kb, vb = _qkv(x, m_i, g[0], diff_w_qkv[j], hd, rope=(pos, hd // 4))
            extra = (diff_lambda[j], diff_subln_g[j].reshape(1, 2 * hd))
            if caches is None:
                x = _attn_prompt("diff", x, m_i, g[1], qb, kb, vb, diff_w_out[j], extra, lam_init)
            else:
                kc = caches["diff_k"][j].reshape(b, -1, d)
                vc = caches["diff_v"][j].reshape(b, -1, 2 * hd)
                x = _attn_stream("diff", x, m_i, g[1], qb, kc, vc, kb, vb, diff_w_out[j], extra, lam_init)
            dk.append(k.reshape(b, s, DIFF_HEADS, 2, hd))
            dv.append(v.reshape(b, s, DIFF_HEADS, 2 * hd))
        elif kind == 2:
            prev = jnp.zeros((b, 2, d), F32) if caches is None else caches["sconv"][j]
            x, st = _sconv(x, m_i, g[0], g[1], prev, sc_w_in[j], sc_conv_w[j], sc_w_out[j])
            scs.append(st)
        else:
            hd = d // SB_HEADS
            k, v, qb, kb, vb = _qkv(x, m_i, g[0], sb_w_qkv[j], hd)
            if caches is None:
                x = _attn_prompt("stick", x, m_i, g[1], qb, kb, vb, sb_w_out[j])
            else:
                kc = caches["sb_k"][j].reshape(b, -1, d)
                vc = caches["sb_v"][j].reshape(b, -1, d)
                x = _attn_stream("stick", x, m_i, g[1], qb, kc, vc, kb, vb, sb_w_out[j])
            sbk.append(k.reshape(b, s, SB_HEADS, hd))
            sbv.append(v.reshape(b, s, SB_HEADS, hd))
        prev = jnp.zeros((b, 2, dff), F32) if caches is None else caches["ffn"][i]
        x, st = _ffn(x, m_i, g[2], g[3], prev, ffn_w_in[i], ffn_conv_w[i], ffn_conv_b[i], ffn_w_out[i])
        ffs.append(st)
    return (x, jnp.stack(gm_v), jnp.stack(dk), jnp.stack(dv), jnp.stack(scs),
            jnp.stack(sbk), jnp.stack(sbv), jnp.stack(ffs))


def kernel(x_prompt, x_sample, cache_diff_k, cache_diff_v, state_sconv, cache_sb_k, cache_sb_v, state_ffn_conv, c_prompt, c_sample, ada_w, ada_b, norm_g, gm_w_in, gm_ln_g, gm_ln_b, gm_ws, gm_bs, gm_w_out, diff_w_qkv, diff_lambda, diff_subln_g, diff_w_out, sc_w_in, sc_conv_w, sc_w_out, sb_w_qkv, sb_w_out, ffn_w_in, ffn_conv_w, ffn_conv_b, ffn_w_out):
    bp = x_prompt.shape[0]
    past = cache_diff_k.shape[2]
    cast = lambda w: w.astype(BF16)
    wts = (norm_g, cast(gm_w_in), gm_ln_g, gm_ln_b, gm_ws, gm_bs, cast(gm_w_out),
           cast(diff_w_qkv), diff_lambda, diff_subln_g, cast(diff_w_out),
           cast(sc_w_in), sc_conv_w, cast(sc_w_out), cast(sb_w_qkv), cast(sb_w_out),
           cast(ffn_w_in), ffn_conv_w, ffn_conv_b, cast(ffn_w_out))

    mod = _modulation(jnp.concatenate([c_prompt, c_sample], axis=0), ada_w, ada_b)
    mod = jnp.transpose(mod, (0, 2, 1, 3))
    mod_p, mod_s = mod[:, :bp], mod[:, bp:]

    pos_p = jnp.arange(x_prompt.shape[1])
    pos_s = past + jnp.arange(x_sample.shape[1])
    caches = {"diff_k": cache_diff_k, "diff_v": cache_diff_v, "sconv": state_sconv,
              "sb_k": cache_sb_k, "sb_v": cache_sb_v, "ffn": state_ffn_conv}

    (y_p, _, dk_p, dv_p, sc_p, sbk_p, sbv_p, ff_p) = _trunk(x_prompt, mod_p, pos_p, None, wts)
    (y_s, gmv_s, dk_s, dv_s, sc_s, sbk_s, sbv_s, ff_s) = _trunk(x_sample, mod_s, pos_s, caches, wts)
    return (y_p, y_s, gmv_s, dk_p, dv_p, dk_s, dv_s, sc_p, sc_s,
            sbk_p, sbv_p, sbk_s, sbv_s, ff_p, ff_s)
```

```python
import functools
import math

import jax
import jax.numpy as jnp
from jax import lax
from jax.experimental import pallas as pl
from jax.experimental.pallas import tpu as pltpu

F32 = jnp.float32
BF16 = jnp.bfloat16

EPS = 1e-6
CHUNK = 64
N_MIXERS = 4
GM_CHUNK = 128
GM_GROUPS = 8
DIFF_HEADS = 8
ROPE_THETA = 500000.0
SB_HEADS = 16

LANES = 128
CARRY_ROWS = 8
NEG_BIG = -1e30
LOG2E = math.log2(math.e)
STICK_FLOOR = -130.0

ROW_TILE = 512
ATTN_TILE = 256
CACHE_TILE = {"diff": 2048, "stick": 512}
CONV_COLS = 256
HEADS_PER_LOOP = 8
VMEM_LIMIT = 56 * 1024 * 1024


def _params(n_axes):
    return pltpu.CompilerParams(dimension_semantics=("arbitrary",) * n_axes,
                                vmem_limit_bytes=VMEM_LIMIT)


def _const_spec(shape):
    zeros = (0,) * len(shape)
    return pl.BlockSpec(shape, lambda *_: zeros, pipeline_mode=pl.Buffered(1))


def _rms(x, g):
    return x * lax.rsqrt(jnp.mean(x * x, axis=-1, keepdims=True) + EPS) * g


def _prenorm(x, g, scale, shift):
    return _rms(x, g) * (1.0 + scale) + shift


def _dot(a, b):
    return jnp.dot(a, b, preferred_element_type=F32)


def _dot_nt(a, b):
    return lax.dot_general(a, b, (((1,), (1,)), ((), ())), preferred_element_type=F32)


def _mod_kernel(c_ref, w_ref, b_ref, o_ref):
    c = c_ref[...]
    cs = (c * jax.nn.sigmoid(c)).astype(BF16)
    o_ref[0] = _dot(cs, w_ref[0].astype(BF16)) + b_ref[0]


def _modulation(c_all, ada_w, ada_b):
    depth, d, _ = ada_w.shape
    rows = c_all.shape[0]
    out = pl.pallas_call(
        _mod_kernel,
        grid=(depth, 6),
        in_specs=[
            pl.BlockSpec((rows, d), lambda i, j: (0, 0)),
            pl.BlockSpec((1, d, d), lambda i, j: (i, 0, j)),
            pl.BlockSpec((1, 1, d), lambda i, j: (i * 6 + j, 0, 0)),
        ],
        out_specs=pl.BlockSpec((1, rows, d), lambda i, j: (i * 6 + j, 0, 0)),
        out_shape=jax.ShapeDtypeStruct((depth * 6, rows, d), F32),
        compiler_params=_params(2),
    )(c_all, ada_w, ada_b.reshape(depth * 6, 1, d))
    return out.reshape(depth, 6, rows, d)


def _row_tiling(b, s):
    tm = min(ROW_TILE, s)
    bt = max(1, min(b, ROW_TILE // tm))
    while b % bt:
        bt -= 1
    return bt, tm


def _tile_spec(bt, tm, width):
    return pl.BlockSpec((bt, tm, width), lambda i, j: (i, j, 0))


def _batch_spec(bt, rows, width):
    return pl.BlockSpec((bt, rows, width), lambda i, j: (i, 0, 0))


def _prenorm_rows(x3, g, mod3, scale_row, shift_row):
    bt, tm, d = x3.shape
    h3 = _prenorm(x3, g, mod3[:, scale_row:scale_row + 1], mod3[:, shift_row:shift_row + 1])
    return h3.reshape(bt * tm, d).astype(BF16)


def _gated_residual(x3, mod3, gate_row, m, gpost):
    return x3 + mod3[:, gate_row:gate_row + 1] * _rms(m, gpost).reshape(x3.shape)


def _conv3(val, cw, hist, lo, width):
    bt, rows, _ = hist.shape
    tm = rows - CARRY_ROWS
    cols = slice(lo, lo + width)
    hist[:, CARRY_ROWS:, cols] = val.reshape(bt, tm, width)
    back1 = hist[:, CARRY_ROWS - 1:CARRY_ROWS - 1 + tm, cols].reshape(bt * tm, width)
    back2 = hist[:, CARRY_ROWS - 2:CARRY_ROWS - 2 + tm, cols].reshape(bt * tm, width)
    hist[:, 0:CARRY_ROWS, cols] = hist[:, tm:tm + CARRY_ROWS, cols]
    return cw[2:3] * val + cw[1:2] * back1 + cw[0:1] * back2


def _init_hist(hist, prev_ref):
    @pl.when(pl.program_id(1) == 0)
    def _():
        hist[:, 0:CARRY_ROWS, :] = jnp.zeros((hist.shape[0], CARRY_ROWS, hist.shape[2]), F32)
        hist[:, CARRY_ROWS - 2:CARRY_ROWS, :] = prev_ref[...]


def _ffn_kernel(x_ref, mod_ref, gpre_ref, gpost_ref, prev_ref, win_ref, cw_ref, cb_ref, wout_ref,
                y_ref, st_ref, hist, act, *, dff, fc):
    _init_hist(hist, prev_ref)
    x3 = x_ref[...]
    mod3 = mod_ref[...]
    h = _prenorm_rows(x3, gpre_ref[...], mod3, 4, 3)
    for c in range(dff // fc):
        lo = c * fc
        g = _dot(h, win_ref[:, lo:lo + fc])
        u = _dot(h, win_ref[:, dff + lo:dff + lo + fc])
        y = _conv3(g, cw_ref[:, lo:lo + fc], hist, lo, fc) + cb_ref[:, lo:lo + fc]
        act[:, lo:lo + fc] = (y * jax.nn.sigmoid(y) * u).astype(BF16)
    st_ref[...] = hist[:, CARRY_ROWS - 2:CARRY_ROWS, :]
    y_ref[...] = _gated_residual(x3, mod3, 5, _dot(act[...], wout_ref[...]), gpost_ref[...])


def _ffn(x, mod, gpre, gpost, prev, w_in, conv_w, conv_b, w_out):
    b, s, d = x.shape
    dff = w_out.shape[0]
    bt, tm = _row_tiling(b, s)
    kern = functools.partial(_ffn_kernel, dff=dff, fc=CONV_COLS)
    return pl.pallas_call(
        kern,
        grid=(b // bt, s // tm),
        in_specs=[
            _tile_spec(bt, tm, d),
            _batch_spec(bt, 6, d),
            _const_spec((1, d)),
            _const_spec((1, d)),
            _batch_spec(bt, 2, dff),
            _const_spec((d, 2 * dff)),
            _const_spec((3, dff)),
            _const_spec((1, dff)),
            _const_spec((dff, d)),
        ],
        out_specs=[_tile_spec(bt, tm, d), _batch_spec(bt, 2, dff)],
        out_shape=[
            jax.ShapeDtypeStruct((b, s, d), F32),
            jax.ShapeDtypeStruct((b, 2, dff), F32),
        ],
        scratch_shapes=[pltpu.VMEM((bt, CARRY_ROWS + tm, dff), F32), pltpu.VMEM((bt * tm, dff), BF16)],
        compiler_params=_params(2),
    )(x, mod, gpre, gpost, prev, w_in, conv_w, conv_b.reshape(1, dff), w_out)


def _sconv_kernel(x_ref, mod_ref, gpre_ref, gpost_ref, prev_ref, win_ref, cw_ref, wout_ref,
                  y_ref, st_ref, hist, act, *, dsc, fc):
    _init_hist(hist, prev_ref)
    x3 = x_ref[...]
    mod3 = mod_ref[...]
    h = _prenorm_rows(x3, gpre_ref[...], mod3, 1, 0)
    for c in range(dsc // fc):
        lo = c * fc
        b_gate = _dot(h, win_ref[:, lo:lo + fc])
        c_gate = _dot(h, win_ref[:, dsc + lo:dsc + lo + fc])
        xin = _dot(h, win_ref[:, 2 * dsc + lo:2 * dsc + lo + fc])
        y = _conv3(c_gate * xin, cw_ref[:, lo:lo + fc], hist, lo, fc)
        act[:, lo:lo + fc] = (b_gate * y).astype(BF16)
    st_ref[...] = hist[:, CARRY_ROWS - 2:CARRY_ROWS, :]
    y_ref[...] = _gated_residual(x3, mod3, 2, _dot(act[...], wout_ref[...]), gpost_ref[...])


def _sconv(x, mod, gpre, gpost, prev, w_in, conv_w, w_out):
    b, s, d = x.shape
    dsc = w_out.shape[0]
    bt, tm = _row_tiling(b, s)
    kern = functools.partial(_sconv_kernel, dsc=dsc, fc=CONV_COLS)
    return pl.pallas_call(
        kern,
        grid=(b // bt, s // tm),
        in_specs=[
            _tile_spec(bt, tm, d),
            _batch_spec(bt, 6, d),
            _const_spec((1, d)),
            _const_spec((1, d)),
            _batch_spec(bt, 2, dsc),
            _const_spec((d, 3 * dsc)),
            _const_spec((3, dsc)),
            _const_spec((dsc, d)),
        ],
        out_specs=[_tile_spec(bt, tm, d), _batch_spec(bt, 2, dsc)],
        out_shape=[
            jax.ShapeDtypeStruct((b, s, d), F32),
            jax.ShapeDtypeStruct((b, 2, dsc), F32),
        ],
        scratch_shapes=[pltpu.VMEM((bt, CARRY_ROWS + tm, dsc), F32), pltpu.VMEM((bt * tm, dsc), BF16)],
        compiler_params=_params(2),
    )(x, mod, gpre, gpost, prev, w_in, conv_w, w_out)


def _gmlp_kernel(x_ref, mod_ref, gpre_ref, gpost_ref, win_ref, lng_ref, lnb_ref, ws_ref, bsf_ref,
                 wout_ref, y_ref, v_ref, mixbuf, *, span, dgm):
    x3 = x_ref[...]
    mod3 = mod_ref[...]
    h = _prenorm_rows(x3, gpre_ref[...], mod3, 1, 0)
    rows = h.shape[0]
    u = jax.nn.gelu(_dot(h, win_ref[:, 0:dgm]))
    v = jax.nn.gelu(_dot(h, win_ref[:, dgm:2 * dgm]))
    vc = v - jnp.mean(v, axis=-1, keepdims=True)
    v = vc * lax.rsqrt(jnp.mean(vc * vc, axis=-1, keepdims=True) + EPS) * lng_ref[...] + lnb_ref[...]
    v_ref[...] = v.reshape(v_ref.shape)
    mix_dtype = BF16 if span % 16 == 0 and span >= LANES else F32
    vm = v.astype(mix_dtype)
    gw = dgm // GM_GROUPS
    row = lax.broadcasted_iota(jnp.int32, (span, span), 0)
    col = lax.broadcasted_iota(jnp.int32, (span, span), 1)
    for g in range(GM_GROUPS):
        w = jnp.where(row >= col, ws_ref[g], 0.0).astype(mix_dtype)
        for r in range(rows // span):
            mixbuf[r * span:(r + 1) * span, g * gw:(g + 1) * gw] = (
                _dot(w, vm[r * span:(r + 1) * span, g * gw:(g + 1) * gw]) + bsf_ref[:, g * gw:(g + 1) * gw])
    m = _dot((u * mixbuf[...]).astype(BF16), wout_ref[...])
    y_ref[...] = _gated_residual(x3, mod3, 2, m, gpost_ref[...])


def _gmlp(x, mod, gpre, gpost, w_in, ln_g, ln_b, ws, bs, w_out):
    b, s, d = x.shape
    dgm = w_out.shape[0]
    span = GM_CHUNK if s >= GM_CHUNK else s
    bt, tm = _row_tiling(b, s)
    ws_l = ws[:, :span, :span]
    bs_tile = jnp.repeat(bs[:, :span].T, dgm // GM_GROUPS, axis=1)
    kern = functools.partial(_gmlp_kernel, span=span, dgm=dgm)
    return pl.pallas_call(
        kern,
        grid=(b // bt, s // tm),
        in_specs=[
            _tile_spec(bt, tm, d),
            _batch_spec(bt, 6, d),
            _const_spec((1, d)),
            _const_spec((1, d)),
            _const_spec((d, 2 * dgm)),
            _const_spec((1, dgm)),
            _const_spec((1, dgm)),
            _const_spec((GM_GROUPS, span, span)),
            _const_spec((span, dgm)),
            _const_spec((dgm, d)),
        ],
        out_specs=[_tile_spec(bt, tm, d), _tile_spec(bt, tm, dgm)],
        out_shape=[
            jax.ShapeDtypeStruct((b, s, d), F32),
            jax.ShapeDtypeStruct((b, s, dgm), F32),
        ],
        scratch_shapes=[pltpu.VMEM((bt * tm, dgm), F32)],
        compiler_params=_params(2),
    )(x, mod, gpre, gpost, w_in, ln_g.reshape(1, dgm), ln_b.reshape(1, dgm), ws_l, bs_tile, w_out)


def _rotate(t, cos, sin_fwd, sin_bwd, rot_half):
    outs = []
    for s in range(t.shape[1] // LANES):
        ts = t[:, s * LANES:(s + 1) * LANES]
        ahead = pltpu.roll(ts, LANES - rot_half, 1)
        behind = pltpu.roll(ts, rot_half, 1)
        outs.append(ts * cos + ahead * sin_fwd + behind * sin_bwd)
    return jnp.concatenate(outs, axis=1)


def _qkv_kernel(*refs, d, qscale, rot_half):
    if rot_half:
        (x_ref, mod_ref, gpre_ref, w_ref, cos_ref, sf_ref, sb_ref,
         k_ref, v_ref, qb_ref, kb_ref, vb_ref) = refs
    else:
        x_ref, mod_ref, gpre_ref, w_ref, k_ref, v_ref, qb_ref, kb_ref, vb_ref = refs
    x3 = x_ref[...]
    bt = x3.shape[0]
    h = _prenorm_rows(x3, gpre_ref[...], mod_ref[...], 1, 0)
    q = _dot(h, w_ref[:, 0:d])
    k = _dot(h, w_ref[:, d:2 * d])
    v = _dot(h, w_ref[:, 2 * d:3 * d])
    if rot_half:
        cos, sf, sb = (jnp.concatenate([t[...]] * bt, axis=0) for t in (cos_ref, sf_ref, sb_ref))
        q = _rotate(q, cos, sf, sb, rot_half)
        k = _rotate(k, cos, sf, sb, rot_half)
    k_ref[...] = k.reshape(x3.shape)
    v_ref[...] = v.reshape(x3.shape)
    qb_ref[...] = (q * qscale).astype(BF16).reshape(x3.shape)
    kb_ref[...] = k.astype(BF16).reshape(x3.shape)
    vb_ref[...] = v.astype(BF16).reshape(x3.shape)


def _rope_tables(pos, head_dim, rot_dim):
    half = rot_dim // 2
    inv = ROPE_THETA ** (-jnp.arange(0, rot_dim, 2, dtype=F32) / rot_dim)
    ang = pos.astype(F32)[:, None] * inv[None, :]
    cos, sin = jnp.cos(ang), jnp.sin(ang)
    n = pos.shape[0]
    pad = jnp.zeros((n, head_dim - rot_dim), F32)
    zeros = jnp.zeros((n, half), F32)
    cos_t = jnp.concatenate([cos, cos, pad + 1.0], axis=1)
    sf_t = jnp.concatenate([-sin, zeros, pad], axis=1)
    sb_t = jnp.concatenate([zeros, sin, pad], axis=1)
    rep = LANES // head_dim
    return tuple(jnp.tile(t, (1, rep)) for t in (cos_t, sf_t, sb_t))


def _qkv(x, mod, gpre, w_qkv, head_dim, rope=None):
    b, s, d = x.shape
    bt, tm = _row_tiling(b, s)
    rot_half = 0
    tables = ()
    table_specs = []
    if rope is not None:
        pos, rot_dim = rope
        rot_half = rot_dim // 2
        tables = _rope_tables(pos, head_dim, rot_dim)
        table_specs = [pl.BlockSpec((tm, LANES), lambda i, j: (j, 0))] * 3
    kern = functools.partial(_qkv_kernel, d=d, qscale=head_dim ** -0.5 * LOG2E, rot_half=rot_half)
    tile = _tile_spec(bt, tm, d)
    return pl.pallas_call(
        kern,
        grid=(b // bt, s // tm),
        in_specs=[tile, _batch_spec(bt, 6, d), _const_spec((1, d)), _const_spec((d, 3 * d))] + table_specs,
        out_specs=[tile] * 5,
        out_shape=[jax.ShapeDtypeStruct((b, s, d), F32)] * 2 + [jax.ShapeDtypeStruct((b, s, d), BF16)] * 3,
        compiler_params=_params(2),
    )(x, mod, gpre, w_qkv, *tables)


def _stack_halves(qh):
    lane = lax.broadcasted_iota(jnp.int32, qh.shape, 1)
    zero = jnp.zeros_like(qh)
    return jnp.concatenate([jnp.where(lane < LANES // 2, qh, zero),
                            jnp.where(lane >= LANES // 2, qh, zero)], axis=0)


def _lane_tiles(x):
    return [x[:, t * LANES:(t + 1) * LANES] for t in range(x.shape[1] // LANES)]


def _row_bcast(col):
    return jnp.broadcast_to(col, (col.shape[0], LANES))


def _softmax_block(q2, ks, vs, state, h, mask=None, first=False):
    m_ref, l_ref, acc_ref = state
    s = _dot_nt(q2, ks)
    if mask is not None:
        s = jnp.where(mask, s, NEG_BIG)
    tiles = _lane_tiles(s)
    m_blk = _row_bcast(jnp.max(functools.reduce(jnp.maximum, tiles), axis=-1, keepdims=True))
    m_new = m_blk if first else jnp.maximum(m_ref[h], m_blk)
    ps = [jnp.exp2(t - m_new) for t in tiles]
    p_sum = functools.reduce(jnp.add, ps)
    pv = _dot(jnp.concatenate(ps, axis=1).astype(BF16), vs)
    if first:
        l_ref[h] = p_sum
        acc_ref[h] = pv
    else:
        alpha = jnp.exp2(m_ref[h] - m_new)
        l_ref[h] = alpha * l_ref[h] + p_sum
        acc_ref[h] = alpha * acc_ref[h] + pv
    m_ref[h] = m_new


def _diff_lambda(lam_ref, lam_init):
    lp = lam_ref[...]
    s1 = jnp.sum(lp[0:1] * lp[1:2], axis=-1, keepdims=True)
    s2 = jnp.sum(lp[2:3] * lp[3:4], axis=-1, keepdims=True)
    return jnp.exp(s1) - jnp.exp(s2) + lam_init


def _diff_head_out(state, h, rows, lam, sg, lam_init):
    _, l_ref, acc_ref = state
    l, acc = jnp.sum(l_ref[h], axis=-1, keepdims=True), acc_ref[h]
    o = acc[:rows] / l[:rows] - lam * (acc[rows:] / l[rows:])
    return (_rms(o, sg) * (1.0 - lam_init)).astype(BF16)


def _suffix_ones(n):
    row = lax.broadcasted_iota(jnp.int32, (n, n), 0)
    col = lax.broadcasted_iota(jnp.int32, (n, n), 1)
    return jnp.where(row > col, 1.0, 0.0).astype(BF16)


def _stick_block(q2, ks, vs, state, h, ones_u, mask=None, first=False):
    run_ref, acc_ref = state
    z = _dot_nt(q2, ks)
    log_beta = jnp.minimum(z, 0.0) - jnp.log2(1.0 + jnp.exp2(-jnp.abs(z)))
    log_keep = log_beta - z
    if mask is not None:
        log_keep = jnp.where(mask, log_keep, 0.0)
    expo = log_beta + _dot(log_keep.astype(BF16), ones_u)
    if not first:
        run = run_ref[h]
        expo = expo + jnp.concatenate([run] * (z.shape[1] // LANES), axis=1)
    w = jnp.exp2(expo)
    if mask is not None:
        w = jnp.where(mask, w, 0.0)
    pv = _dot(w.astype(BF16), vs)
    keep_sum = functools.reduce(jnp.add, _lane_tiles(log_keep))
    keep_sum = _row_bcast(jnp.sum(keep_sum, axis=-1, keepdims=True))
    if first:
        acc_ref[h] = pv
        run_ref[h] = keep_sum
    else:
        acc_ref[h] = acc_ref[h] + pv
        run_ref[h] = run + keep_sum


def _stick_alive(state, heads):
    top = functools.reduce(jnp.maximum, [state[0][h] for h in heads])
    return (jnp.max(top) > STICK_FLOOR).astype(jnp.int32)


def _stick_head_out(state, h, rows):
    acc = state[1][h]
    lane = lax.broadcasted_iota(jnp.int32, (rows, LANES), 1)
    return jnp.where(lane < LANES // 2, acc[:rows], acc[rows:]).astype(BF16)


def _attn_state(kind, n_slabs, rows):
    n = 3 if kind == "diff" else 2
    return [pltpu.VMEM((n_slabs, rows, LANES), F32)] * n


def _rel_index(tq, tk):
    qi = lax.broadcasted_iota(jnp.int32, (2 * tq, tk), 0)
    qi = jnp.where(qi >= tq, qi - tq, qi)
    ki = lax.broadcasted_iota(jnp.int32, (2 * tq, tk), 1)
    return qi, ki


def _attn_prompt_kernel(*refs, kind, tq, d, lam_init):
    if kind == "diff":
        x_ref, mod_ref, gpost_ref, q_ref, k_ref, v_ref, wout_ref, lam_ref, sg_ref, y_ref, obuf, *state = refs
        lam = _diff_lambda(lam_ref, lam_init)
        sg = sg_ref[...]
    else:
        x_ref, mod_ref, gpost_ref, q_ref, k_ref, v_ref, wout_ref, y_ref, obuf, *state = refs
        ones_u = _suffix_ones(tq)
    j = pl.program_id(1)
    qi, ki = _rel_index(tq, tq)
    if kind == "diff":
        diag_mask = (ki // CHUNK) <= (qi // CHUNK)
    else:
        diag_mask = ki < qi

    def kv(kb, sl):
        start = pl.multiple_of(kb * tq, tq)
        return k_ref[0, pl.ds(start, tq), sl], v_ref[0, pl.ds(start, tq), sl]

    n_slabs = d // LANES
    for h0 in range(0, n_slabs, HEADS_PER_LOOP):
        heads = range(h0, min(h0 + HEADS_PER_LOOP, n_slabs))
        slabs = {h: slice(h * LANES, (h + 1) * LANES) for h in heads}
        q2s = {h: _stack_halves(q_ref[0, :, slabs[h]]) for h in heads}
        if kind == "diff":
            for h in heads:
                _softmax_block(q2s[h], *kv(j, slabs[h]), state, h, diag_mask, first=True)

            @pl.loop(0, j)
            def _(kb):
                for h in heads:
                    _softmax_block(q2s[h], *kv(kb, slabs[h]), state, h)

            for h in heads:
                obuf[:, slabs[h]] = _diff_head_out(state, h, tq, lam, sg, lam_init)
        else:
            for h in heads:
                _stick_block(q2s[h], *kv(j, slabs[h]), state, h, ones_u, diag_mask, first=True)

            def older_block(c):
                i, _ = c
                for h in heads:
                    _stick_block(q2s[h], *kv(j - 1 - i, slabs[h]), state, h, ones_u)
                return i + 1, _stick_alive(state, heads)

            lax.while_loop(lambda c: jnp.logical_and(c[0] < j, c[1] > 0), older_block,
                           (jnp.int32(0), _stick_alive(state, heads)))

            for h in heads:
                obuf[:, slabs[h]] = _stick_head_out(state, h, tq)

    m = _dot(obuf[...], wout_ref[...])
    y_ref[0] = x_ref[0] + mod_ref[0][2:3] * _rms(m, gpost_ref[...])


def _attn_prompt(kind, x, mod, gpost, qb, kb, vb, w_out, extra=(), lam_init=0.0):
    b, s, d = x.shape
    tq = min(ATTN_TILE, s)
    kern = functools.partial(_attn_prompt_kernel, kind=kind, tq=tq, d=d, lam_init=lam_init)
    tile = pl.BlockSpec((1, tq, d), lambda i, j: (i, j, 0))
    whole = pl.BlockSpec((1, s, d), lambda i, j: (i, 0, 0))
    return pl.pallas_call(
        kern,
        grid=(b, s // tq),
        in_specs=[tile, pl.BlockSpec((1, 6, d), lambda i, j: (i, 0, 0)), _const_spec((1, d)),
                  tile, whole, whole, _const_spec((d, d))] + [_const_spec(e.shape) for e in extra],
        out_specs=tile,
        out_shape=jax.ShapeDtypeStruct((b, s, d), F32),
        scratch_shapes=[pltpu.VMEM((tq, d), BF16)] + _attn_state(kind, d // LANES, 2 * tq),
        compiler_params=_params(2),
    )(x, mod, gpost, qb, kb, vb, w_out, *extra)


def _attn_stream_kernel(*refs, kind, s_new, past, tkc, d, lam_init, v_rows_by_head):
    if kind == "diff":
        (x_ref, mod_ref, gpost_ref, q_ref, kc_ref, vc_ref, kn_ref, vn_ref, wout_ref, lam_ref, sg_ref,
         y_ref, obuf, *state) = refs
        lam = _diff_lambda(lam_ref, lam_init)
        sg = sg_ref[...]
    else:
        x_ref, mod_ref, gpost_ref, q_ref, kc_ref, vc_ref, kn_ref, vn_ref, wout_ref, y_ref, obuf, *state = refs
        ones_new = _suffix_ones(kn_ref.shape[1])
        ones_cache = _suffix_ones(tkc)
    qi, ki = _rel_index(s_new, kn_ref.shape[1])
    if kind == "diff":
        new_mask = (((past + ki) // CHUNK) <= ((past + qi) // CHUNK)) & (ki < s_new)
    else:
        new_mask = (ki < qi) & (ki < s_new)
    n_cache = past // tkc

    heads = range(d // LANES)
    slabs = {h: slice(h * LANES, (h + 1) * LANES) for h in heads}
    q2s = {h: _stack_halves(q_ref[0, :, slabs[h]]) for h in heads}

    def cache_kv(kb, sl):
        rows = slice(kb * tkc, (kb + 1) * tkc)
        if v_rows_by_head:
            n_heads = d // LANES
            vs = vc_ref[0, pl.ds(kb * tkc * n_heads + sl.start // LANES, tkc, stride=n_heads), :]
        else:
            vs = vc_ref[0, rows, sl]
        return kc_ref[0, rows, sl].astype(BF16), vs.astype(BF16)

    if kind == "diff":
        for h in heads:
            _softmax_block(q2s[h], kn_ref[0, :, slabs[h]], vn_ref[0, :, slabs[h]], state, h, new_mask, first=True)
        for kb in range(n_cache):
            for h in heads:
                _softmax_block(q2s[h], *cache_kv(kb, slabs[h]), state, h)
        for h in heads:
            obuf[:, slabs[h]] = _diff_head_out(state, h, s_new, lam, sg, lam_init)
    else:
        for h in heads:
            _stick_block(q2s[h], kn_ref[0, :, slabs[h]], vn_ref[0, :, slabs[h]], state, h, ones_new, new_mask,
                         first=True)
        for kb in reversed(range(n_cache)):
            @pl.when(_stick_alive(state, heads) > 0)
            def _():
                for h in heads:
                    _stick_block(q2s[h], *cache_kv(kb, slabs[h]), state, h, ones_cache)
        for h in heads:
            obuf[:, slabs[h]] = _stick_head_out(state, h, s_new)

    m = _dot(obuf[...], wout_ref[...])
    y_ref[0] = x_ref[0] + mod_ref[0][2:3] * _rms(m, gpost_ref[...])


def _attn_stream(kind, x, mod, gpost, qb, k_cache, v_cache, kb_new, vb_new, w_out, extra=(), lam_init=0.0):
    b, s, d = x.shape
    v_rows_by_head = v_cache.shape[2] == LANES
    past = k_cache.shape[1]
    tkc = min(CACHE_TILE[kind], past)
    pad = (-s) % LANES
    kn = jnp.pad(kb_new, ((0, 0), (0, pad), (0, 0)))
    vn = jnp.pad(vb_new, ((0, 0), (0, pad), (0, 0)))
    kern = functools.partial(_attn_stream_kernel, kind=kind, s_new=s, past=past, tkc=tkc, d=d, lam_init=lam_init,
                             v_rows_by_head=v_rows_by_head)
    tile = pl.BlockSpec((1, s, d), lambda i: (i, 0, 0))
    cache = pl.BlockSpec((1, past, d), lambda i: (i, 0, 0))
    v_spec = pl.BlockSpec((1,) + v_cache.shape[1:], lambda i: (i, 0, 0))
    new = pl.BlockSpec((1, s + pad, d), lambda i: (i, 0, 0))
    return pl.pallas_call(
        kern,
        grid=(b,),
        in_specs=[tile, pl.BlockSpec((1, 6, d), lambda i: (i, 0, 0)), _const_spec((1, d)),
                  tile, cache, v_spec, new, new, _const_spec((d, d))] + [_const_spec(e.shape) for e in extra],
        out_specs=tile,
        out_shape=jax.ShapeDtypeStruct((b, s, d), F32),
        scratch_shapes=[pltpu.VMEM((s, d), BF16)] + _attn_state(kind, d // LANES, 2 * s),
        compiler_params=_params(1),
    )(x, mod, gpost, qb, k_cache, v_cache, kn, vn, w_out, *extra)


def _trunk(x, mod, pos, caches, wts):
    (norm_g, gm_w_in, gm_ln_g, gm_ln_b, gm_ws, gm_bs, gm_w_out,
     diff_w_qkv, diff_lambda, diff_subln_g, diff_w_out,
     sc_w_in, sc_conv_w, sc_w_out, sb_w_qkv, sb_w_out,
     ffn_w_in, ffn_conv_w, ffn_conv_b, ffn_w_out) = wts
    b, s, d = x.shape
    depth = norm_g.shape[0]
    dff = ffn_w_out.shape[1]
    gm_v, dk, dv, scs, sbk, sbv, ffs = [], [], [], [], [], [], []
    for i in range(depth):
        kind, j = i % N_MIXERS, i // N_MIXERS
        m_i = mod[i]
        g = [norm_g[i, n].reshape(1, d) for n in range(4)]
        if kind == 0:
            x, v_rows = _gmlp(x, m_i, g[0], g[1], gm_w_in[j], gm_ln_g[j], gm_ln_b[j], gm_ws[j], gm_bs[j],
                              gm_w_out[j])
            gm_v.append(v_rows)
        elif kind == 1:
            hd = d // (2 * DIFF_HEADS)
            lam_init = 0.8 - 0.6 * math.exp(-0.3 * i)
            k, v, qb, kb, vb = _qkv(x, m_i, g[0], diff_w_qkv[j], hd, rope=(pos, hd // 4))
            extra = (diff_lambda[j], diff_subln_g[j].reshape(1, 2 * hd))
            if caches is None:
                x = _attn_prompt("diff", x, m_i, g[1], qb, kb, vb, diff_w_out[j], extra, lam_init)
            else:
                kc = caches["diff_k"][j].reshape(b, -1, d)
                vc = caches["diff_v"][j].reshape(b, -1, 2 * hd)
                x = _attn_stream("diff", x, m_i, g[1], qb, kc, vc, kb, vb, diff_w_out[j], extra, lam_init)
            dk.append(k.reshape(b, s, DIFF_HEADS, 2, hd))
            dv.append(v.reshape(b, s, DIFF_HEADS, 2 * hd))
        elif kind == 2:
            prev = jnp.zeros((b, 2, d), F32) if caches is None else caches["sconv"][j]
            x, st = _sconv(x, m_i, g[0], g[1], prev, sc_w_in[j], sc_conv_w[j], sc_w_out[j])
            scs.append(st)
        else:
            hd = d // SB_HEADS
            k, v, qb, kb, vb = _qkv(x, m_i, g[0], sb_w_qkv[j], hd)
            if caches is None:
                x = _attn_prompt("stick", x, m_i, g[1], qb, kb, vb, sb_w_out[j])
            else:
                kc = caches["sb_k"][j].reshape(b, -1, d)
                vc = caches["sb_v"][j].reshape(b, -1, d)
                x = _attn_stream("stick", x, m_i, g[1], qb, kc, vc, kb, vb, sb_w_out[j])
            sbk.append(k.reshape(b, s, SB_HEADS, hd))
            sbv.append(v.reshape(b, s, SB_HEADS, hd))
        prev = jnp.zeros((b, 2, dff), F32) if caches is None else caches["ffn"][i]
        x, st = _ffn(x, m_i, g[2], g[3], prev, ffn_w_in[i], ffn_conv_w[i], ffn_conv_b[i], ffn_w_out[i])
        ffs.append(st)
    return (x, jnp.stack(gm_v), jnp.stack(dk), jnp.stack(dv), jnp.stack(scs),
            jnp.stack(sbk), jnp.stack(sbv), jnp.stack(ffs))


def kernel(x_prompt, x_sample, cache_diff_k, cache_diff_v, state_sconv, cache_sb_k, cache_sb_v, state_ffn_conv, c_prompt, c_sample, ada_w, ada_b, norm_g, gm_w_in, gm_ln_g, gm_ln_b, gm_ws, gm_bs, gm_w_out, diff_w_qkv, diff_lambda, diff_subln_g, diff_w_out, sc_w_in, sc_conv_w, sc_w_out, sb_w_qkv, sb_w_out, ffn_w_in, ffn_conv_w, ffn_conv_b, ffn_w_out):
    bp = x_prompt.shape[0]
    past = cache_diff_k.shape[2]
    cast = lambda w: w.astype(BF16)
    wts = (norm_g, cast(gm_w_in), gm_ln_g, gm_ln_b, gm_ws, gm_bs, cast(gm_w_out),
           cast(diff_w_qkv), diff_lambda, diff_subln_g, cast(diff_w_out),
           cast(sc_w_in), sc_conv_w, cast(sc_w_out), cast(sb_w_qkv), cast(sb_w_out),
           cast(ffn_w_in), ffn_conv_w, ffn_conv_b, cast(ffn_w_out))

    mod = _modulation(jnp.concatenate([c_prompt, c_sample], axis=0), ada_w, ada_b)
    mod = jnp.transpose(mod, (0, 2, 1, 3))
    mod_p, mod_s = mod[:, :bp], mod[:, bp:]

    pos_p = jnp.arange(x_prompt.shape[1])
    pos_s = past + jnp.arange(x_sample.shape[1])
    caches = {"diff_k": cache_diff_k, "diff_v": cache_diff_v, "sconv": state_sconv,
              "sb_k": cache_sb_k, "sb_v": cache_sb_v, "ffn": state_ffn_conv}

    (y_p, _, dk_p, dv_p, sc_p, sbk_p, sbv_p, ff_p) = _trunk(x_prompt, mod_p, pos_p, None, wts)
    (y_s, gmv_s, dk_s, dv_s, sc_s, sbk_s, sbv_s, ff_s) = _trunk(x_sample, mod_s, pos_s, caches, wts)
    return (y_p, y_s, gmv_s, dk_p, dv_p, dk_s, dv_s, sc_p, sc_s,
            sbk_p, sbv_p, sbk_s, sbv_s, ff_p, ff_s)
```

```python
import functools
import math

import jax
import jax.numpy as jnp
from jax import lax
from jax.experimental import pallas as pl
from jax.experimental.pallas import tpu as pltpu

F32 = jnp.float32
BF16 = jnp.bfloat16

EPS = 1e-6
CHUNK = 64
N_MIXERS = 4
GM_CHUNK = 128
GM_GROUPS = 8
DIFF_HEADS = 8
ROPE_THETA = 500000.0
SB_HEADS = 16

LANES = 128
CARRY_ROWS = 8
NEG_BIG = -1e30
LOG2E = math.log2(math.e)
STICK_FLOOR = -130.0

ROW_TILE = 512
ATTN_TILE = 256
CACHE_TILE = {"diff": 1024, "stick": 512}
CONV_COLS = 256
HEADS_PER_LOOP = 8
VMEM_LIMIT = 56 * 1024 * 1024


def _params(n_axes):
    return pltpu.CompilerParams(dimension_semantics=("arbitrary",) * n_axes,
                                vmem_limit_bytes=VMEM_LIMIT)


def _const_spec(shape):
    zeros = (0,) * len(shape)
    return pl.BlockSpec(shape, lambda *_: zeros, pipeline_mode=pl.Buffered(1))


def _rms(x, g):
    return x * lax.rsqrt(jnp.mean(x * x, axis=-1, keepdims=True) + EPS) * g


def _prenorm(x, g, scale, shift):
    return _rms(x, g) * (1.0 + scale) + shift


def _dot(a, b):
    return jnp.dot(a, b, preferred_element_type=F32)


def _dot_nt(a, b):
    return lax.dot_general(a, b, (((1,), (1,)), ((), ())), preferred_element_type=F32)


def _mod_kernel(c_ref, w_ref, b_ref, o_ref):
    c = c_ref[...]
    cs = (c * jax.nn.sigmoid(c)).astype(BF16)
    o_ref[0] = _dot(cs, w_ref[0].astype(BF16)) + b_ref[0]


def _modulation(c_all, ada_w, ada_b):
    depth, d, _ = ada_w.shape
    rows = c_all.shape[0]
    out = pl.pallas_call(
        _mod_kernel,
        grid=(depth, 6),
        in_specs=[
            pl.BlockSpec((rows, d), lambda i, j: (0, 0)),
            pl.BlockSpec((1, d, d), lambda i, j: (i, 0, j)),
            pl.BlockSpec((1, 1, d), lambda i, j: (i * 6 + j, 0, 0)),
        ],
        out_specs=pl.BlockSpec((1, rows, d), lambda i, j: (i * 6 + j, 0, 0)),
        out_shape=jax.ShapeDtypeStruct((depth * 6, rows, d), F32),
        compiler_params=_params(2),
    )(c_all, ada_w, ada_b.reshape(depth * 6, 1, d))
    return out.reshape(depth, 6, rows, d)


def _row_tiling(b, s):
    tm = min(ROW_TILE, s)
    bt = max(1, min(b, ROW_TILE // tm))
    while b % bt:
        bt -= 1
    return bt, tm


def _tile_spec(bt, tm, width):
    return pl.BlockSpec((bt, tm, width), lambda i, j: (i, j, 0))


def _batch_spec(bt, rows, width):
    return pl.BlockSpec((bt, rows, width), lambda i, j: (i, 0, 0))


def _prenorm_rows(x3, g, mod3, scale_row, shift_row):
    bt, tm, d = x3.shape
    h3 = _prenorm(x3, g, mod3[:, scale_row:scale_row + 1], mod3[:, shift_row:shift_row + 1])
    return h3.reshape(bt * tm, d).astype(BF16)


def _gated_residual(x3, mod3, gate_row, m, gpost):
    return x3 + mod3[:, gate_row:gate_row + 1] * _rms(m, gpost).reshape(x3.shape)


def _conv3(val, cw, hist, lo, width):
    bt, rows, _ = hist.shape
    tm = rows - CARRY_ROWS
    cols = slice(lo, lo + width)
    hist[:, CARRY_ROWS:, cols] = val.reshape(bt, tm, width)
    back1 = hist[:, CARRY_ROWS - 1:CARRY_ROWS - 1 + tm, cols].reshape(bt * tm, width)
    back2 = hist[:, CARRY_ROWS - 2:CARRY_ROWS - 2 + tm, cols].reshape(bt * tm, width)
    hist[:, 0:CARRY_ROWS, cols] = hist[:, tm:tm + CARRY_ROWS, cols]
    return cw[2:3] * val + cw[1:2] * back1 + cw[0:1] * back2


def _init_hist(hist, prev_ref):
    @pl.when(pl.program_id(1) == 0)
    def _():
        hist[:, 0:CARRY_ROWS, :] = jnp.zeros((hist.shape[0], CARRY_ROWS, hist.shape[2]), F32)
        hist[:, CARRY_ROWS - 2:CARRY_ROWS, :] = prev_ref[...]


def _ffn_kernel(x_ref, mod_ref, gpre_ref, gpost_ref, prev_ref, win_ref, cw_ref, cb_ref, wout_ref,
                y_ref, st_ref, hist, act, *, dff, fc):
    _init_hist(hist, prev_ref)
    x3 = x_ref[...]
    mod3 = mod_ref[...]
    h = _prenorm_rows(x3, gpre_ref[...], mod3, 4, 3)
    for c in range(dff // fc):
        lo = c * fc
        g = _dot(h, win_ref[:, lo:lo + fc])
        u = _dot(h, win_ref[:, dff + lo:dff + lo + fc])
        y = _conv3(g, cw_ref[:, lo:lo + fc], hist, lo, fc) + cb_ref[:, lo:lo + fc]
        act[:, lo:lo + fc] = (y * jax.nn.sigmoid(y) * u).astype(BF16)
    st_ref[...] = hist[:, CARRY_ROWS - 2:CARRY_ROWS, :]
    y_ref[...] = _gated_residual(x3, mod3, 5, _dot(act[...], wout_ref[...]), gpost_ref[...])


def _ffn(x, mod, gpre, gpost, prev, w_in, conv_w, conv_b, w_out):
    b, s, d = x.shape
    dff = w_out.shape[0]
    bt, tm = _row_tiling(b, s)
    kern = functools.partial(_ffn_kernel, dff=dff, fc=CONV_COLS)
    return pl.pallas_call(
        kern,
        grid=(b // bt, s // tm),
        in_specs=[
            _tile_spec(bt, tm, d),
            _batch_spec(bt, 6, d),
            _const_spec((1, d)),
            _const_spec((1, d)),
            _batch_spec(bt, 2, dff),
            _const_spec((d, 2 * dff)),
            _const_spec((3, dff)),
            _const_spec((1, dff)),
            _const_spec((dff, d)),
        ],
        out_specs=[_tile_spec(bt, tm, d), _batch_spec(bt, 2, dff)],
        out_shape=[
            jax.ShapeDtypeStruct((b, s, d), F32),
            jax.ShapeDtypeStruct((b, 2, dff), F32),
        ],
        scratch_shapes=[pltpu.VMEM((bt, CARRY_ROWS + tm, dff), F32), pltpu.VMEM((bt * tm, dff), BF16)],
        compiler_params=_params(2),
    )(x, mod, gpre, gpost, prev, w_in, conv_w, conv_b.reshape(1, dff), w_out)


def _sconv_kernel(x_ref, mod_ref, gpre_ref, gpost_ref, prev_ref, win_ref, cw_ref, wout_ref,
                  y_ref, st_ref, hist, act, *, dsc, fc):
    _init_hist(hist, prev_ref)
    x3 = x_ref[...]
    mod3 = mod_ref[...]
    h = _prenorm_rows(x3, gpre_ref[...], mod3, 1, 0)
    for c in range(dsc // fc):
        lo = c * fc
        b_gate = _dot(h, win_ref[:, lo:lo + fc])
        c_gate = _dot(h, win_ref[:, dsc + lo:dsc + lo + fc])
        xin = _dot(h, win_ref[:, 2 * dsc + lo:2 * dsc + lo + fc])
        y = _conv3(c_gate * xin, cw_ref[:, lo:lo + fc], hist, lo, fc)
        act[:, lo:lo + fc] = (b_gate * y).astype(BF16)
    st_ref[...] = hist[:, CARRY_ROWS - 2:CARRY_ROWS, :]
    y_ref[...] = _gated_residual(x3, mod3, 2, _dot(act[...], wout_ref[...]), gpost_ref[...])


def _sconv(x, mod, gpre, gpost, prev, w_in, conv_w, w_out):
    b, s, d = x.shape
    dsc = w_out.shape[0]
    bt, tm = _row_tiling(b, s)
    kern = functools.partial(_sconv_kernel, dsc=dsc, fc=CONV_COLS)
    return pl.pallas_call(
        kern,
        grid=(b // bt, s // tm),
        in_specs=[
            _tile_spec(bt, tm, d),
            _batch_spec(bt, 6, d),
            _const_spec((1, d)),
            _const_spec((1, d)),
            _batch_spec(bt, 2, dsc),
            _const_spec((d, 3 * dsc)),
            _const_spec((3, dsc)),
            _const_spec((dsc, d)),
        ],
        out_specs=[_tile_spec(bt, tm, d), _batch_spec(bt, 2, dsc)],
        out_shape=[
            jax.ShapeDtypeStruct((b, s, d), F32),
            jax.ShapeDtypeStruct((b, 2, dsc), F32),
        ],
        scratch_shapes=[pltpu.VMEM((bt, CARRY_ROWS + tm, dsc), F32), pltpu.VMEM((bt * tm, dsc), BF16)],
        compiler_params=_params(2),
    )(x, mod, gpre, gpost, prev, w_in, conv_w, w_out)


def _gmlp_kernel(x_ref, mod_ref, gpre_ref, gpost_ref, win_ref, lng_ref, lnb_ref, ws_ref, bsf_ref,
                 wout_ref, y_ref, v_ref, mixbuf, *, span, dgm):
    x3 = x_ref[...]
    mod3 = mod_ref[...]
    h = _prenorm_rows(x3, gpre_ref[...], mod3, 1, 0)
    rows = h.shape[0]
    u = jax.nn.gelu(_dot(h, win_ref[:, 0:dgm]))
    v = jax.nn.gelu(_dot(h, win_ref[:, dgm:2 * dgm]))
    vc = v - jnp.mean(v, axis=-1, keepdims=True)
    v = vc * lax.rsqrt(jnp.mean(vc * vc, axis=-1, keepdims=True) + EPS) * lng_ref[...] + lnb_ref[...]
    v_ref[...] = v.reshape(v_ref.shape)
    mix_dtype = BF16 if span % 16 == 0 and span >= LANES else F32
    vm = v.astype(mix_dtype)
    gw = dgm // GM_GROUPS
    row = lax.broadcasted_iota(jnp.int32, (span, span), 0)
    col = lax.broadcasted_iota(jnp.int32, (span, span), 1)
    for g in range(GM_GROUPS):
        w = jnp.where(row >= col, ws_ref[g], 0.0).astype(mix_dtype)
        for r in range(rows // span):
            mixbuf[r * span:(r + 1) * span, g * gw:(g + 1) * gw] = (
                _dot(w, vm[r * span:(r + 1) * span, g * gw:(g + 1) * gw]) + bsf_ref[:, g * gw:(g + 1) * gw])
    m = _dot((u * mixbuf[...]).astype(BF16), wout_ref[...])
    y_ref[...] = _gated_residual(x3, mod3, 2, m, gpost_ref[...])


def _gmlp(x, mod, gpre, gpost, w_in, ln_g, ln_b, ws, bs, w_out):
    b, s, d = x.shape
    dgm = w_out.shape[0]
    span = GM_CHUNK if s >= GM_CHUNK else s
    bt, tm = _row_tiling(b, s)
    ws_l = ws[:, :span, :span]
    bs_tile = jnp.repeat(bs[:, :span].T, dgm // GM_GROUPS, axis=1)
    kern = functools.partial(_gmlp_kernel, span=span, dgm=dgm)
    return pl.pallas_call(
        kern,
        grid=(b // bt, s // tm),
        in_specs=[
            _tile_spec(bt, tm, d),
            _batch_spec(bt, 6, d),
            _const_spec((1, d)),
            _const_spec((1, d)),
            _const_spec((d, 2 * dgm)),
            _const_spec((1, dgm)),
            _const_spec((1, dgm)),
            _const_spec((GM_GROUPS, span, span)),
            _const_spec((span, dgm)),
            _const_spec((dgm, d)),
        ],
        out_specs=[_tile_spec(bt, tm, d), _tile_spec(bt, tm, dgm)],
        out_shape=[
            jax.ShapeDtypeStruct((b, s, d), F32),
            jax.ShapeDtypeStruct((b, s, dgm), F32),
        ],
        scratch_shapes=[pltpu.VMEM((bt * tm, dgm), F32)],
        compiler_params=_params(2),
    )(x, mod, gpre, gpost, w_in, ln_g.reshape(1, dgm), ln_b.reshape(1, dgm), ws_l, bs_tile, w_out)


def _rotate(t, cos, sin_fwd, sin_bwd, rot_half):
    outs = []
    for s in range(t.shape[1] // LANES):
        ts = t[:, s * LANES:(s + 1) * LANES]
        ahead = pltpu.roll(ts, LANES - rot_half, 1)
        behind = pltpu.roll(ts, rot_half, 1)
        outs.append(ts * cos + ahead * sin_fwd + behind * sin_bwd)
    return jnp.concatenate(outs, axis=1)


def _qkv_kernel(*refs, d, qscale, rot_half):
    if rot_half:
        (x_ref, mod_ref, gpre_ref, w_ref, cos_ref, sf_ref, sb_ref,
         k_ref, v_ref, qb_ref, kb_ref, vb_ref) = refs
    else:
        x_ref, mod_ref, gpre_ref, w_ref, k_ref, v_ref, qb_ref, kb_ref, vb_ref = refs
    x3 = x_ref[...]
    bt = x3.shape[0]
    h = _prenorm_rows(x3, gpre_ref[...], mod_ref[...], 1, 0)
    q = _dot(h, w_ref[:, 0:d])
    k = _dot(h, w_ref[:, d:2 * d])
    v = _dot(h, w_ref[:, 2 * d:3 * d])
    if rot_half:
        cos, sf, sb = (jnp.concatenate([t[...]] * bt, axis=0) for t in (cos_ref, sf_ref, sb_ref))
        q = _rotate(q, cos, sf, sb, rot_half)
        k = _rotate(k, cos, sf, sb, rot_half)
    k_ref[...] = k.reshape(x3.shape)
    v_ref[...] = v.reshape(x3.shape)
    qb_ref[...] = (q * qscale).astype(BF16).reshape(x3.shape)
    kb_ref[...] = k.astype(BF16).reshape(x3.shape)
    vb_ref[...] = v.astype(BF16).reshape(x3.shape)


def _rope_tables(pos, head_dim, rot_dim):
    half = rot_dim // 2
    inv = ROPE_THETA ** (-jnp.arange(0, rot_dim, 2, dtype=F32) / rot_dim)
    ang = pos.astype(F32)[:, None] * inv[None, :]
    cos, sin = jnp.cos(ang), jnp.sin(ang)
    n = pos.shape[0]
    pad = jnp.zeros((n, head_dim - rot_dim), F32)
    zeros = jnp.zeros((n, half), F32)
    cos_t = jnp.concatenate([cos, cos, pad + 1.0], axis=1)
    sf_t = jnp.concatenate([-sin, zeros, pad], axis=1)
    sb_t = jnp.concatenate([zeros, sin, pad], axis=1)
    rep = LANES // head_dim
    return tuple(jnp.tile(t, (1, rep)) for t in (cos_t, sf_t, sb_t))


def _qkv(x, mod, gpre, w_qkv, head_dim, rope=None):
    b, s, d = x.shape
    bt, tm = _row_tiling(b, s)
    rot_half = 0
    tables = ()
    table_specs = []
    if rope is not None:
        pos, rot_dim = rope
        rot_half = rot_dim // 2
        tables = _rope_tables(pos, head_dim, rot_dim)
        table_specs = [pl.BlockSpec((tm, LANES), lambda i, j: (j, 0))] * 3
    kern = functools.partial(_qkv_kernel, d=d, qscale=head_dim ** -0.5 * LOG2E, rot_half=rot_half)
    tile = _tile_spec(bt, tm, d)
    return pl.pallas_call(
        kern,
        grid=(b // bt, s // tm),
        in_specs=[tile, _batch_spec(bt, 6, d), _const_spec((1, d)), _const_spec((d, 3 * d))] + table_specs,
        out_specs=[tile] * 5,
        out_shape=[jax.ShapeDtypeStruct((b, s, d), F32)] * 2 + [jax.ShapeDtypeStruct((b, s, d), BF16)] * 3,
        compiler_params=_params(2),
    )(x, mod, gpre, w_qkv, *tables)


def _stack_halves(qh):
    lane = lax.broadcasted_iota(jnp.int32, qh.shape, 1)
    zero = jnp.zeros_like(qh)
    return jnp.concatenate([jnp.where(lane < LANES // 2, qh, zero),
                            jnp.where(lane >= LANES // 2, qh, zero)], axis=0)


def _lane_tiles(x):
    return [x[:, t * LANES:(t + 1) * LANES] for t in range(x.shape[1] // LANES)]


def _row_bcast(col):
    return jnp.broadcast_to(col, (col.shape[0], LANES))


def _softmax_block(q2, ks, vs, state, h, mask=None, first=False):
    m_ref, l_ref, acc_ref = state
    s = _dot_nt(q2, ks)
    if mask is not None:
        s = jnp.where(mask, s, NEG_BIG)
    tiles = _lane_tiles(s)
    m_blk = _row_bcast(jnp.max(functools.reduce(jnp.maximum, tiles), axis=-1, keepdims=True))
    m_new = m_blk if first else jnp.maximum(m_ref[h], m_blk)
    ps = [jnp.exp2(t - m_new) for t in tiles]
    p_sum = functools.reduce(jnp.add, ps)
    pv = _dot(jnp.concatenate(ps, axis=1).astype(BF16), vs)
    if first:
        l_ref[h] = p_sum
        acc_ref[h] = pv
    else:
        alpha = jnp.exp2(m_ref[h] - m_new)
        l_ref[h] = alpha * l_ref[h] + p_sum
        acc_ref[h] = alpha * acc_ref[h] + pv
    m_ref[h] = m_new


def _diff_lambda(lam_ref, lam_init):
    lp = lam_ref[...]
    s1 = jnp.sum(lp[0:1] * lp[1:2], axis=-1, keepdims=True)
    s2 = jnp.sum(lp[2:3] * lp[3:4], axis=-1, keepdims=True)
    return jnp.exp(s1) - jnp.exp(s2) + lam_init


def _diff_head_out(state, h, rows, lam, sg, lam_init):
    _, l_ref, acc_ref = state
    l, acc = jnp.sum(l_ref[h], axis=-1, keepdims=True), acc_ref[h]
    o = acc[:rows] / l[:rows] - lam * (acc[rows:] / l[rows:])
    return (_rms(o, sg) * (1.0 - lam_init)).astype(BF16)


def _suffix_ones(n):
    row = lax.broadcasted_iota(jnp.int32, (n, n), 0)
    col = lax.broadcasted_iota(jnp.int32, (n, n), 1)
    return jnp.where(row > col, 1.0, 0.0).astype(BF16)


def _stick_block(q2, ks, vs, state, h, ones_u, mask=None, first=False):
    run_ref, acc_ref = state
    z = _dot_nt(q2, ks)
    log_beta = jnp.minimum(z, 0.0) - jnp.log2(1.0 + jnp.exp2(-jnp.abs(z)))
    log_keep = log_beta - z
    if mask is not None:
        log_keep = jnp.where(mask, log_keep, 0.0)
    expo = log_beta + _dot(log_keep.astype(BF16), ones_u)
    if not first:
        run = run_ref[h]
        expo = expo + jnp.concatenate([run] * (z.shape[1] // LANES), axis=1)
    w = jnp.exp2(expo)
    if mask is not None:
        w = jnp.where(mask, w, 0.0)
    pv = _dot(w.astype(BF16), vs)
    keep_sum = functools.reduce(jnp.add, _lane_tiles(log_keep))
    keep_sum = _row_bcast(jnp.sum(keep_sum, axis=-1, keepdims=True))
    if first:
        acc_ref[h] = pv
        run_ref[h] = keep_sum
    else:
        acc_ref[h] = acc_ref[h] + pv
        run_ref[h] = run + keep_sum


def _stick_alive(state, heads):
    top = functools.reduce(jnp.maximum, [state[0][h] for h in heads])
    return (jnp.max(top) > STICK_FLOOR).astype(jnp.int32)


def _stick_head_out(state, h, rows):
    acc = state[1][h]
    lane = lax.broadcasted_iota(jnp.int32, (rows, LANES), 1)
    return jnp.where(lane < LANES // 2, acc[:rows], acc[rows:]).astype(BF16)


def _attn_state(kind, n_slabs, rows):
    n = 3 if kind == "diff" else 2
    return [pltpu.VMEM((n_slabs, rows, LANES), F32)] * n


def _rel_index(tq, tk):
    qi = lax.broadcasted_iota(jnp.int32, (2 * tq, tk), 0)
    qi = jnp.where(qi >= tq, qi - tq, qi)
    ki = lax.broadcasted_iota(jnp.int32, (2 * tq, tk), 1)
    return qi, ki


def _attn_prompt_kernel(*refs, kind, tq, d, lam_init):
    if kind == "diff":
        x_ref, mod_ref, gpost_ref, q_ref, k_ref, v_ref, wout_ref, lam_ref, sg_ref, y_ref, obuf, *state = refs
        lam = _diff_lambda(lam_ref, lam_init)
        sg = sg_ref[...]
    else:
        x_ref, mod_ref, gpost_ref, q_ref, k_ref, v_ref, wout_ref, y_ref, obuf, *state = refs
        ones_u = _suffix_ones(tq)
    j = pl.program_id(1)
    qi, ki = _rel_index(tq, tq)
    if kind == "diff":
        diag_mask = (ki // CHUNK) <= (qi // CHUNK)
    else:
        diag_mask = ki < qi

    def kv(kb, sl):
        start = pl.multiple_of(kb * tq, tq)
        return k_ref[0, pl.ds(start, tq), sl], v_ref[0, pl.ds(start, tq), sl]

    n_slabs = d // LANES
    for h0 in range(0, n_slabs, HEADS_PER_LOOP):
        heads = range(h0, min(h0 + HEADS_PER_LOOP, n_slabs))
        slabs = {h: slice(h * LANES, (h + 1) * LANES) for h in heads}
        q2s = {h: _stack_halves(q_ref[0, :, slabs[h]]) for h in heads}
        if kind == "diff":
            for h in heads:
                _softmax_block(q2s[h], *kv(j, slabs[h]), state, h, diag_mask, first=True)

            @pl.loop(0, j)
            def _(kb):
                for h in heads:
                    _softmax_block(q2s[h], *kv(kb, slabs[h]), state, h)

            for h in heads:
                obuf[:, slabs[h]] = _diff_head_out(state, h, tq, lam, sg, lam_init)
        else:
            for h in heads:
                _stick_block(q2s[h], *kv(j, slabs[h]), state, h, ones_u, diag_mask, first=True)

            def older_block(c):
                i, _ = c
                for h in heads:
                    _stick_block(q2s[h], *kv(j - 1 - i, slabs[h]), state, h, ones_u)
                return i + 1, _stick_alive(state, heads)

            lax.while_loop(lambda c: jnp.logical_and(c[0] < j, c[1] > 0), older_block,
                           (jnp.int32(0), _stick_alive(state, heads)))

            for h in heads:
                obuf[:, slabs[h]] = _stick_head_out(state, h, tq)

    m = _dot(obuf[...], wout_ref[...])
    y_ref[0] = x_ref[0] + mod_ref[0][2:3] * _rms(m, gpost_ref[...])


def _attn_prompt(kind, x, mod, gpost, qb, kb, vb, w_out, extra=(), lam_init=0.0):
    b, s, d = x.shape
    tq = min(ATTN_TILE, s)
    kern = functools.partial(_attn_prompt_kernel, kind=kind, tq=tq, d=d, lam_init=lam_init)
    tile = pl.BlockSpec((1, tq, d), lambda i, j: (i, j, 0))
    whole = pl.BlockSpec((1, s, d), lambda i, j: (i, 0, 0))
    return pl.pallas_call(
        kern,
        grid=(b, s // tq),
        in_specs=[tile, pl.BlockSpec((1, 6, d), lambda i, j: (i, 0, 0)), _const_spec((1, d)),
                  tile, whole, whole, _const_spec((d, d))] + [_const_spec(e.shape) for e in extra],
        out_specs=tile,
        out_shape=jax.ShapeDtypeStruct((b, s, d), F32),
        scratch_shapes=[pltpu.VMEM((tq, d), BF16)] + _attn_state(kind, d // LANES, 2 * tq),
        compiler_params=_params(2),
    )(x, mod, gpost, qb, kb, vb, w_out, *extra)


def _stream_kernel(*refs, kind, s_new, past, tkc, d, k_rows, v_rows, lam_init):
    if kind == "diff":
        (x_ref, mod_ref, gpost_ref, q_ref, kc_ref, vc_ref, kn_ref, vn_ref, wout_ref, lam_ref, sg_ref,
         y_ref, obuf, *state) = refs
    else:
        x_ref, mod_ref, gpost_ref, q_ref, kc_ref, vc_ref, kn_ref, vn_ref, wout_ref, y_ref, obuf, *state = refs
    step = pl.program_id(1)
    heads = range(d // LANES)
    slabs = {h: slice(h * LANES, (h + 1) * LANES) for h in heads}
    q2s = {h: _stack_halves(q_ref[0, :, slabs[h]]) for h in heads}

    @pl.when(step == 0)
    def _():
        qi, ki = _rel_index(s_new, kn_ref.shape[1])
        if kind == "diff":
            new_mask = (((past + ki) // CHUNK) <= ((past + qi) // CHUNK)) & (ki < s_new)
        else:
            new_mask = (ki < qi) & (ki < s_new)
            ones_new = _suffix_ones(kn_ref.shape[1])
        for h in heads:
            new_kv = kn_ref[0, :, slabs[h]], vn_ref[0, :, slabs[h]]
            if kind == "diff":
                _softmax_block(q2s[h], *new_kv, state, h, new_mask, first=True)
            else:
                _stick_block(q2s[h], *new_kv, state, h, ones_new, new_mask, first=True)

    def slab(ref, h, rows_per_pos):
        per_slab = rows_per_pos // len(heads)
        parts = [ref[0, pl.ds(per_slab * h + e, tkc, stride=rows_per_pos), :] for e in range(per_slab)]
        return (parts[0] if per_slab == 1 else jnp.concatenate(parts, axis=1)).astype(BF16)

    def cache_block():
        if kind == "stick":
            ones_cache = _suffix_ones(tkc)
        for h in heads:
            ks, vs = slab(kc_ref, h, k_rows), slab(vc_ref, h, v_rows)
            if kind == "diff":
                _softmax_block(q2s[h], ks, vs, state, h)
            else:
                _stick_block(q2s[h], ks, vs, state, h, ones_cache)

    if kind == "diff":
        cache_block()
    else:
        pl.when(_stick_alive(state, heads) > 0)(cache_block)

    @pl.when(step == pl.num_programs(1) - 1)
    def _():
        if kind == "diff":
            lam = _diff_lambda(lam_ref, lam_init)
        for h in heads:
            if kind == "diff":
                obuf[:, slabs[h]] = _diff_head_out(state, h, s_new, lam, sg_ref[...], lam_init)
            else:
                obuf[:, slabs[h]] = _stick_head_out(state, h, s_new)
        m = _dot(obuf[...], wout_ref[...])
        y_ref[0] = x_ref[0] + mod_ref[0][2:3] * _rms(m, gpost_ref[...])


def _attn_stream(kind, x, mod, gpost, qb, k_cache, v_cache, kb_new, vb_new, w_out, extra=(), lam_init=0.0):
    b, s, d = x.shape
    past = k_cache.shape[1]
    k_rows = d // k_cache.shape[-1]
    v_rows = d // v_cache.shape[-1]
    tkc = min(CACHE_TILE[kind], past)
    n_blocks = past // tkc
    pad = (-s) % LANES
    kn = jnp.pad(kb_new, ((0, 0), (0, pad), (0, 0)))
    vn = jnp.pad(vb_new, ((0, 0), (0, pad), (0, 0)))
    kern = functools.partial(_stream_kernel, kind=kind, s_new=s, past=past, tkc=tkc, d=d, k_rows=k_rows,
                             v_rows=v_rows, lam_init=lam_init)
    tile = pl.BlockSpec((1, s, d), lambda i, j: (i, 0, 0))
    k_spec = pl.BlockSpec((1, tkc * k_rows, d // k_rows), lambda i, j: (i, n_blocks - 1 - j, 0))
    v_spec = pl.BlockSpec((1, tkc * v_rows, d // v_rows), lambda i, j: (i, n_blocks - 1 - j, 0))
    new = pl.BlockSpec((1, s + pad, d), lambda i, j: (i, 0, 0))
    return pl.pallas_call(
        kern,
        grid=(b, n_blocks),
        in_specs=[tile, pl.BlockSpec((1, 6, d), lambda i, j: (i, 0, 0)), _const_spec((1, d)),
                  tile, k_spec, v_spec, new, new, _const_spec((d, d))] + [_const_spec(e.shape) for e in extra],
        out_specs=tile,
        out_shape=jax.ShapeDtypeStruct((b, s, d), F32),
        scratch_shapes=[pltpu.VMEM((s, d), BF16)] + _attn_state(kind, d // LANES, 2 * s),
        compiler_params=_params(2),
    )(x, mod, gpost, qb, k_cache.reshape(b, past * k_rows, d // k_rows),
      v_cache.reshape(b, past * v_rows, d // v_rows), kn, vn, w_out, *extra)


def _trunk(x, mod, pos, caches, wts):
    (norm_g, gm_w_in, gm_ln_g, gm_ln_b, gm_ws, gm_bs, gm_w_out,
     diff_w_qkv, diff_lambda, diff_subln_g, diff_w_out,
     sc_w_in, sc_conv_w, sc_w_out, sb_w_qkv, sb_w_out,
     ffn_w_in, ffn_conv_w, ffn_conv_b, ffn_w_out) = wts
    b, s, d = x.shape
    depth = norm_g.shape[0]
    dff = ffn_w_out.shape[1]
    gm_v, dk, dv, scs, sbk, sbv, ffs = [], [], [], [], [], [], []
    for i in range(depth):
        kind, j = i % N_MIXERS, i // N_MIXERS
        m_i = mod[i]
        g = [norm_g[i, n].reshape(1, d) for n in range(4)]
        if kind == 0:
            x, v_rows = _gmlp(x, m_i, g[0], g[1], gm_w_in[j], gm_ln_g[j], gm_ln_b[j], gm_ws[j], gm_bs[j],
                              gm_w_out[j])
            gm_v.append(v_rows)
        elif kind == 1:
            hd = d // (2 * DIFF_HEADS)
            lam_init = 0.8 - 0.6 * math.exp(-0.3 * i)
            k, v, qb, kb, vb = _qkv(x, m_i, g[0], diff_w_qkv[j], hd, rope=(pos, hd // 4))
            extra = (diff_lambda[j], diff_subln_g[j].reshape(1, 2 * hd))
            if caches is None:
                x = _attn_prompt("diff", x, m_i, g[1], qb, kb, vb, diff_w_out[j], extra, lam_init)
            else:
                x = _attn_stream("diff", x, m_i, g[1], qb, caches["diff_k"][j], caches["diff_v"][j], kb, vb,
                                 diff_w_out[j], extra, lam_init)
            dk.append(k.reshape(b, s, DIFF_HEADS, 2, hd))
            dv.append(v.reshape(b, s, DIFF_HEADS, 2 * hd))
        elif kind == 2:
            prev = jnp.zeros((b, 2, d), F32) if caches is None else caches["sconv"][j]
            x, st = _sconv(x, m_i, g[0], g[1], prev, sc_w_in[j], sc_conv_w[j], sc_w_out[j])
            scs.append(st)
        else:
            hd = d // SB_HEADS
            k, v, qb, kb, vb = _qkv(x, m_i, g[0], sb_w_qkv[j], hd)
            if caches is None:
                x = _attn_prompt("stick", x, m_i, g[1], qb, kb, vb, sb_w_out[j])
            else:
                x = _attn_stream("stick", x, m_i, g[1], qb, caches["sb_k"][j], caches["sb_v"][j], kb, vb,
                                 sb_w_out[j])
            sbk.append(k.reshape(b, s, SB_HEADS, hd))
            sbv.append(v.reshape(b, s, SB_HEADS, hd))
        prev = jnp.zeros((b, 2, dff), F32) if caches is None else caches["ffn"][i]
        x, st = _ffn(x, m_i, g[2], g[3], prev, ffn_w_in[i], ffn_conv_w[i], ffn_conv_b[i], ffn_w_out[i])
        ffs.append(st)
    return (x, jnp.stack(gm_v), jnp.stack(dk), jnp.stack(dv), jnp.stack(scs),
            jnp.stack(sbk), jnp.stack(sbv), jnp.stack(ffs))


def kernel(x_prompt, x_sample, cache_diff_k, cache_diff_v, state_sconv, cache_sb_k, cache_sb_v, state_ffn_conv, c_prompt, c_sample, ada_w, ada_b, norm_g, gm_w_in, gm_ln_g, gm_ln_b, gm_ws, gm_bs, gm_w_out, diff_w_qkv, diff_lambda, diff_subln_g, diff_w_out, sc_w_in, sc_conv_w, sc_w_out, sb_w_qkv, sb_w_out, ffn_w_in, ffn_conv_w, ffn_conv_b, ffn_w_out):
    bp = x_prompt.shape[0]
    past = cache_diff_k.shape[2]
    cast = lambda w: w.astype(BF16)
    wts = (norm_g, cast(gm_w_in), gm_ln_g, gm_ln_b, gm_ws, gm_bs, cast(gm_w_out),
           cast(diff_w_qkv), diff_lambda, diff_subln_g, cast(diff_w_out),
           cast(sc_w_in), sc_conv_w, cast(sc_w_out), cast(sb_w_qkv), cast(sb_w_out),
           cast(ffn_w_in), ffn_conv_w, ffn_conv_b, cast(ffn_w_out))

    mod = _modulation(jnp.concatenate([c_prompt, c_sample], axis=0), ada_w, ada_b)
    mod = jnp.transpose(mod, (0, 2, 1, 3))
    mod_p, mod_s = mod[:, :bp], mod[:, bp:]

    pos_p = jnp.arange(x_prompt.shape[1])
    pos_s = past + jnp.arange(x_sample.shape[1])
    caches = {"diff_k": cache_diff_k, "diff_v": cache_diff_v, "sconv": state_sconv,
              "sb_k": cache_sb_k, "sb_v": cache_sb_v, "ffn": state_ffn_conv}

    (y_p, _, dk_p, dv_p, sc_p, sbk_p, sbv_p, ff_p) = _trunk(x_prompt, mod_p, pos_p, None, wts)
    (y_s, gmv_s, dk_s, dv_s, sc_s, sbk_s, sbv_s, ff_s) = _trunk(x_sample, mod_s, pos_s, caches, wts)
    return (y_p, y_s, gmv_s, dk_p, dv_p, dk_s, dv_s, sc_p, sc_s,
            sbk_p, sbv_p, sbk_s, sbv_s, ff_p, ff_s)
```

```python
import functools
import math

import jax
import jax.numpy as jnp
from jax import lax
from jax.experimental import pallas as pl
from jax.experimental.pallas import tpu as pltpu

F32 = jnp.float32
BF16 = jnp.bfloat16

EPS = 1e-6
CHUNK = 64
N_MIXERS = 4
GM_CHUNK = 128
GM_GROUPS = 8
DIFF_HEADS = 8
ROPE_THETA = 500000.0
SB_HEADS = 16

LANES = 128
CARRY_ROWS = 8
NEG_BIG = -1e30
LOG2E = math.log2(math.e)
STICK_FLOOR = -130.0

ROW_TILE = 512
ATTN_TILE = 256
CACHE_TILE = {"diff": 2048, "stick": 512}
CONV_COLS = 256
HEADS_PER_LOOP = 8
VMEM_LIMIT = 56 * 1024 * 1024


def _params(n_axes):
    return pltpu.CompilerParams(dimension_semantics=("arbitrary",) * n_axes,
                                vmem_limit_bytes=VMEM_LIMIT)


def _const_spec(shape):
    zeros = (0,) * len(shape)
    return pl.BlockSpec(shape, lambda *_: zeros, pipeline_mode=pl.Buffered(1))


def _col_block_specs(rows, width, n):
    return [pl.BlockSpec((rows, width // n), lambda *_, k=k: (0, k), pipeline_mode=pl.Buffered(1))
            for k in range(n)]


def _rms(x, g):
    return x * lax.rsqrt(jnp.mean(x * x, axis=-1, keepdims=True) + EPS) * g


def _prenorm(x, g, scale, shift):
    return _rms(x, g) * (1.0 + scale) + shift


def _dot(a, b):
    return jnp.dot(a, b, preferred_element_type=F32)


def _dot_nt(a, b):
    return lax.dot_general(a, b, (((1,), (1,)), ((), ())), preferred_element_type=F32)


def _mod_kernel(c_ref, w_ref, b_ref, o_ref):
    c = c_ref[...]
    cs = (c * jax.nn.sigmoid(c)).astype(BF16)
    o_ref[0] = _dot(cs, w_ref[0].astype(BF16)) + b_ref[0]


def _modulation(c_all, ada_w, ada_b):
    depth, d, _ = ada_w.shape
    rows = c_all.shape[0]
    out = pl.pallas_call(
        _mod_kernel,
        grid=(depth, 6),
        in_specs=[
            pl.BlockSpec((rows, d), lambda i, j: (0, 0)),
            pl.BlockSpec((1, d, d), lambda i, j: (i, 0, j)),
            pl.BlockSpec((1, 1, d), lambda i, j: (i * 6 + j, 0, 0)),
        ],
        out_specs=pl.BlockSpec((1, rows, d), lambda i, j: (i * 6 + j, 0, 0)),
        out_shape=jax.ShapeDtypeStruct((depth * 6, rows, d), F32),
        compiler_params=_params(2),
    )(c_all, ada_w, ada_b.reshape(depth * 6, 1, d))
    return out.reshape(depth, 6, rows, d)


def _row_tiling(b, s):
    tm = min(ROW_TILE, s)
    bt = max(1, min(b, ROW_TILE // tm))
    while b % bt:
        bt -= 1
    return bt, tm


def _tile_spec(bt, tm, width):
    return pl.BlockSpec((bt, tm, width), lambda i, j: (i, j, 0))


def _batch_spec(bt, rows, width):
    return pl.BlockSpec((bt, rows, width), lambda i, j: (i, 0, 0))


def _prenorm_rows(x3, g, mod3, scale_row, shift_row):
    bt, tm, d = x3.shape
    h3 = _prenorm(x3, g, mod3[:, scale_row:scale_row + 1], mod3[:, shift_row:shift_row + 1])
    return h3.reshape(bt * tm, d).astype(BF16)


def _gated_residual(x3, mod3, gate_row, m, gpost):
    return x3 + mod3[:, gate_row:gate_row + 1] * _rms(m, gpost).reshape(x3.shape)


def _conv3(val, cw, hist, lo, width):
    bt, rows, _ = hist.shape
    tm = rows - CARRY_ROWS
    cols = slice(lo, lo + width)
    hist[:, CARRY_ROWS:, cols] = val.reshape(bt, tm, width)
    back1 = hist[:, CARRY_ROWS - 1:CARRY_ROWS - 1 + tm, cols].reshape(bt * tm, width)
    back2 = hist[:, CARRY_ROWS - 2:CARRY_ROWS - 2 + tm, cols].reshape(bt * tm, width)
    hist[:, 0:CARRY_ROWS, cols] = hist[:, tm:tm + CARRY_ROWS, cols]
    return cw[2:3] * val + cw[1:2] * back1 + cw[0:1] * back2


def _init_hist(hist, prev_ref):
    @pl.when(pl.program_id(1) == 0)
    def _():
        hist[:, 0:CARRY_ROWS, :] = jnp.zeros((hist.shape[0], CARRY_ROWS, hist.shape[2]), F32)
        hist[:, CARRY_ROWS - 2:CARRY_ROWS, :] = prev_ref[...]


def _ffn_kernel(x_ref, mod_ref, gpre_ref, gpost_ref, prev_ref, wg_ref, wu_ref, cw_ref, cb_ref, wout_ref,
                y_ref, st_ref, hist, act, *, dff, fc):
    _init_hist(hist, prev_ref)
    x3 = x_ref[...]
    mod3 = mod_ref[...]
    h = _prenorm_rows(x3, gpre_ref[...], mod3, 4, 3)
    for c in range(dff // fc):
        lo = c * fc
        g = _dot(h, wg_ref[:, lo:lo + fc])
        u = _dot(h, wu_ref[:, lo:lo + fc])
        y = _conv3(g, cw_ref[:, lo:lo + fc], hist, lo, fc) + cb_ref[:, lo:lo + fc]
        act[:, lo:lo + fc] = (y * jax.nn.sigmoid(y) * u).astype(BF16)
    st_ref[...] = hist[:, CARRY_ROWS - 2:CARRY_ROWS, :]
    y_ref[...] = _gated_residual(x3, mod3, 5, _dot(act[...], wout_ref[...]), gpost_ref[...])


def _ffn(x, mod, gpre, gpost, prev, w_in, conv_w, conv_b, w_out):
    b, s, d = x.shape
    dff = w_out.shape[0]
    bt, tm = _row_tiling(b, s)
    kern = functools.partial(_ffn_kernel, dff=dff, fc=CONV_COLS)
    return pl.pallas_call(
        kern,
        grid=(b // bt, s // tm),
        in_specs=[
            _tile_spec(bt, tm, d),
            _batch_spec(bt, 6, d),
            _const_spec((1, d)),
            _const_spec((1, d)),
            _batch_spec(bt, 2, dff),
            *_col_block_specs(d, 2 * dff, 2),
            _const_spec((3, dff)),
            _const_spec((1, dff)),
            _const_spec((dff, d)),
        ],
        out_specs=[_tile_spec(bt, tm, d), _batch_spec(bt, 2, dff)],
        out_shape=[
            jax.ShapeDtypeStruct((b, s, d), F32),
            jax.ShapeDtypeStruct((b, 2, dff), F32),
        ],
        scratch_shapes=[pltpu.VMEM((bt, CARRY_ROWS + tm, dff), F32), pltpu.VMEM((bt * tm, dff), BF16)],
        compiler_params=_params(2),
    )(x, mod, gpre, gpost, prev, w_in, w_in, conv_w, conv_b.reshape(1, dff), w_out)


def _sconv_kernel(x_ref, mod_ref, gpre_ref, gpost_ref, prev_ref, wb_ref, wc_ref, wx_ref, cw_ref, wout_ref,
                  y_ref, st_ref, hist, act, *, dsc, fc):
    _init_hist(hist, prev_ref)
    x3 = x_ref[...]
    mod3 = mod_ref[...]
    h = _prenorm_rows(x3, gpre_ref[...], mod3, 1, 0)
    for c in range(dsc // fc):
        lo = c * fc
        b_gate = _dot(h, wb_ref[:, lo:lo + fc])
        c_gate = _dot(h, wc_ref[:, lo:lo + fc])
        xin = _dot(h, wx_ref[:, lo:lo + fc])
        y = _conv3(c_gate * xin, cw_ref[:, lo:lo + fc], hist, lo, fc)
        act[:, lo:lo + fc] = (b_gate * y).astype(BF16)
    st_ref[...] = hist[:, CARRY_ROWS - 2:CARRY_ROWS, :]
    y_ref[...] = _gated_residual(x3, mod3, 2, _dot(act[...], wout_ref[...]), gpost_ref[...])


def _sconv(x, mod, gpre, gpost, prev, w_in, conv_w, w_out):
    b, s, d = x.shape
    dsc = w_out.shape[0]
    bt, tm = _row_tiling(b, s)
    kern = functools.partial(_sconv_kernel, dsc=dsc, fc=CONV_COLS)
    return pl.pallas_call(
        kern,
        grid=(b // bt, s // tm),
        in_specs=[
            _tile_spec(bt, tm, d),
            _batch_spec(bt, 6, d),
            _const_spec((1, d)),
            _const_spec((1, d)),
            _batch_spec(bt, 2, dsc),
            *_col_block_specs(d, 3 * dsc, 3),
            _const_spec((3, dsc)),
            _const_spec((dsc, d)),
        ],
        out_specs=[_tile_spec(bt, tm, d), _batch_spec(bt, 2, dsc)],
        out_shape=[
            jax.ShapeDtypeStruct((b, s, d), F32),
            jax.ShapeDtypeStruct((b, 2, dsc), F32),
        ],
        scratch_shapes=[pltpu.VMEM((bt, CARRY_ROWS + tm, dsc), F32), pltpu.VMEM((bt * tm, dsc), BF16)],
        compiler_params=_params(2),
    )(x, mod, gpre, gpost, prev, w_in, w_in, w_in, conv_w, w_out)


def _gmlp_kernel(x_ref, mod_ref, gpre_ref, gpost_ref, wu_ref, wv_ref, lng_ref, lnb_ref, ws_ref, bsf_ref,
                 wout_ref, y_ref, v_ref, mixbuf, *, span, dgm):
    x3 = x_ref[...]
    mod3 = mod_ref[...]
    h = _prenorm_rows(x3, gpre_ref[...], mod3, 1, 0)
    rows = h.shape[0]
    u = jax.nn.gelu(_dot(h, wu_ref[...]))
    v = jax.nn.gelu(_dot(h, wv_ref[...]))
    vc = v - jnp.mean(v, axis=-1, keepdims=True)
    v = vc * lax.rsqrt(jnp.mean(vc * vc, axis=-1, keepdims=True) + EPS) * lng_ref[...] + lnb_ref[...]
    v_ref[...] = v.reshape(v_ref.shape)
    mix_dtype = BF16 if span % 16 == 0 and span >= LANES else F32
    vm = v.astype(mix_dtype)
    gw = dgm // GM_GROUPS
    row = lax.broadcasted_iota(jnp.int32, (span, span), 0)
    col = lax.broadcasted_iota(jnp.int32, (span, span), 1)
    for g in range(GM_GROUPS):
        w = jnp.where(row >= col, ws_ref[g], 0.0).astype(mix_dtype)
        for r in range(rows // span):
            mixbuf[r * span:(r + 1) * span, g * gw:(g + 1) * gw] = (
                _dot(w, vm[r * span:(r + 1) * span, g * gw:(g + 1) * gw]) + bsf_ref[:, g * gw:(g + 1) * gw])
    m = _dot((u * mixbuf[...]).astype(BF16), wout_ref[...])
    y_ref[...] = _gated_residual(x3, mod3, 2, m, gpost_ref[...])


def _gmlp(x, mod, gpre, gpost, w_in, ln_g, ln_b, ws, bs, w_out):
    b, s, d = x.shape
    dgm = w_out.shape[0]
    span = GM_CHUNK if s >= GM_CHUNK else s
    bt, tm = _row_tiling(b, s)
    ws_l = ws[:, :span, :span]
    bs_tile = jnp.repeat(bs[:, :span].T, dgm // GM_GROUPS, axis=1)
    kern = functools.partial(_gmlp_kernel, span=span, dgm=dgm)
    return pl.pallas_call(
        kern,
        grid=(b // bt, s // tm),
        in_specs=[
            _tile_spec(bt, tm, d),
            _batch_spec(bt, 6, d),
            _const_spec((1, d)),
            _const_spec((1, d)),
            *_col_block_specs(d, 2 * dgm, 2),
            _const_spec((1, dgm)),
            _const_spec((1, dgm)),
            _const_spec((GM_GROUPS, span, span)),
            _const_spec((span, dgm)),
            _const_spec((dgm, d)),
        ],
        out_specs=[_tile_spec(bt, tm, d), _tile_spec(bt, tm, dgm)],
        out_shape=[
            jax.ShapeDtypeStruct((b, s, d), F32),
            jax.ShapeDtypeStruct((b, s, dgm), F32),
        ],
        scratch_shapes=[pltpu.VMEM((bt * tm, dgm), F32)],
        compiler_params=_params(2),
    )(x, mod, gpre, gpost, w_in, w_in, ln_g.reshape(1, dgm), ln_b.reshape(1, dgm), ws_l, bs_tile, w_out)


def _rotate(t, cos, sin_fwd, sin_bwd, rot_half):
    outs = []
    for s in range(t.shape[1] // LANES):
        ts = t[:, s * LANES:(s + 1) * LANES]
        ahead = pltpu.roll(ts, LANES - rot_half, 1)
        behind = pltpu.roll(ts, rot_half, 1)
        outs.append(ts * cos + ahead * sin_fwd + behind * sin_bwd)
    return jnp.concatenate(outs, axis=1)


def _qkv_kernel(*refs, qscale, rot_half):
    if rot_half:
        (x_ref, mod_ref, gpre_ref, wq_ref, wk_ref, wv_ref, cos_ref, sf_ref, sb_ref,
         k_ref, v_ref, qb_ref, kb_ref, vb_ref) = refs
    else:
        x_ref, mod_ref, gpre_ref, wq_ref, wk_ref, wv_ref, k_ref, v_ref, qb_ref, kb_ref, vb_ref = refs
    x3 = x_ref[...]
    bt = x3.shape[0]
    h = _prenorm_rows(x3, gpre_ref[...], mod_ref[...], 1, 0)
    q = _dot(h, wq_ref[...])
    k = _dot(h, wk_ref[...])
    v = _dot(h, wv_ref[...])
    if rot_half:
        cos, sf, sb = (jnp.concatenate([t[...]] * bt, axis=0) for t in (cos_ref, sf_ref, sb_ref))
        q = _rotate(q, cos, sf, sb, rot_half)
        k = _rotate(k, cos, sf, sb, rot_half)
    k_ref[...] = k.reshape(x3.shape)
    v_ref[...] = v.reshape(x3.shape)
    qb_ref[...] = (q * qscale).astype(BF16).reshape(x3.shape)
    kb_ref[...] = k.astype(BF16).reshape(x3.shape)
    vb_ref[...] = v.astype(BF16).reshape(x3.shape)


def _rope_tables(pos, head_dim, rot_dim):
    half = rot_dim // 2
    inv = ROPE_THETA ** (-jnp.arange(0, rot_dim, 2, dtype=F32) / rot_dim)
    ang = pos.astype(F32)[:, None] * inv[None, :]
    cos, sin = jnp.cos(ang), jnp.sin(ang)
    n = pos.shape[0]
    pad = jnp.zeros((n, head_dim - rot_dim), F32)
    zeros = jnp.zeros((n, half), F32)
    cos_t = jnp.concatenate([cos, cos, pad + 1.0], axis=1)
    sf_t = jnp.concatenate([-sin, zeros, pad], axis=1)
    sb_t = jnp.concatenate([zeros, sin, pad], axis=1)
    rep = LANES // head_dim
    return tuple(jnp.tile(t, (1, rep)) for t in (cos_t, sf_t, sb_t))


def _qkv(x, mod, gpre, w_qkv, head_dim, rope=None):
    b, s, d = x.shape
    bt, tm = _row_tiling(b, s)
    rot_half = 0
    tables = ()
    table_specs = []
    if rope is not None:
        pos, rot_dim = rope
        rot_half = rot_dim // 2
        tables = _rope_tables(pos, head_dim, rot_dim)
        table_specs = [pl.BlockSpec((tm, LANES), lambda i, j: (j, 0))] * 3
    kern = functools.partial(_qkv_kernel, qscale=head_dim ** -0.5 * LOG2E, rot_half=rot_half)
    tile = _tile_spec(bt, tm, d)
    return pl.pallas_call(
        kern,
        grid=(b // bt, s // tm),
        in_specs=[tile, _batch_spec(bt, 6, d), _const_spec((1, d)), *_col_block_specs(d, 3 * d, 3)] + table_specs,
        out_specs=[tile] * 5,
        out_shape=[jax.ShapeDtypeStruct((b, s, d), F32)] * 2 + [jax.ShapeDtypeStruct((b, s, d), BF16)] * 3,
        compiler_params=_params(2),
    )(x, mod, gpre, w_qkv, w_qkv, w_qkv, *tables)


def _stack_halves(qh):
    lane = lax.broadcasted_iota(jnp.int32, qh.shape, 1)
    zero = jnp.zeros_like(qh)
    return jnp.concatenate([jnp.where(lane < LANES // 2, qh, zero),
                            jnp.where(lane >= LANES // 2, qh, zero)], axis=0)


def _lane_tiles(x):
    return [x[:, t * LANES:(t + 1) * LANES] for t in range(x.shape[1] // LANES)]


def _row_bcast(col):
    return jnp.broadcast_to(col, (col.shape[0], LANES))


def _softmax_block(q2, ks, vs, state, h, mask=None, first=False):
    m_ref, l_ref, acc_ref = state
    s = _dot_nt(q2, ks)
    if mask is not None:
        s = jnp.where(mask, s, NEG_BIG)
    tiles = _lane_tiles(s)
    m_blk = _row_bcast(jnp.max(functools.reduce(jnp.maximum, tiles), axis=-1, keepdims=True))
    m_new = m_blk if first else jnp.maximum(m_ref[h], m_blk)
    ps = [jnp.exp2(t - m_new) for t in tiles]
    p_sum = functools.reduce(jnp.add, ps)
    pv = _dot(jnp.concatenate(ps, axis=1).astype(BF16), vs)
    if first:
        l_ref[h] = p_sum
        acc_ref[h] = pv
    else:
        alpha = jnp.exp2(m_ref[h] - m_new)
        l_ref[h] = alpha * l_ref[h] + p_sum
        acc_ref[h] = alpha * acc_ref[h] + pv
    m_ref[h] = m_new


def _diff_lambda(lam_ref, lam_init):
    lp = lam_ref[...]
    s1 = jnp.sum(lp[0:1] * lp[1:2], axis=-1, keepdims=True)
    s2 = jnp.sum(lp[2:3] * lp[3:4], axis=-1, keepdims=True)
    return jnp.exp(s1) - jnp.exp(s2) + lam_init


def _diff_head_out(state, h, rows, lam, sg, lam_init):
    _, l_ref, acc_ref = state
    l, acc = jnp.sum(l_ref[h], axis=-1, keepdims=True), acc_ref[h]
    o = acc[:rows] / l[:rows] - lam * (acc[rows:] / l[rows:])
    return (_rms(o, sg) * (1.0 - lam_init)).astype(BF16)


def _suffix_ones(n):
    row = lax.broadcasted_iota(jnp.int32, (n, n), 0)
    col = lax.broadcasted_iota(jnp.int32, (n, n), 1)
    return jnp.where(row > col, 1.0, 0.0).astype(BF16)


def _stick_block(q2, ks, vs, state, h, ones_u, mask=None, first=False):
    run_ref, acc_ref = state
    z = _dot_nt(q2, ks)
    log_beta = jnp.minimum(z, 0.0) - jnp.log2(1.0 + jnp.exp2(-jnp.abs(z)))
    log_keep = log_beta - z
    if mask is not None:
        log_keep = jnp.where(mask, log_keep, 0.0)
    expo = log_beta + _dot(log_keep.astype(BF16), ones_u)
    if not first:
        run = run_ref[h]
        expo = expo + jnp.concatenate([run] * (z.shape[1] // LANES), axis=1)
    w = jnp.exp2(expo)
    if mask is not None:
        w = jnp.where(mask, w, 0.0)
    pv = _dot(w.astype(BF16), vs)
    keep_sum = functools.reduce(jnp.add, _lane_tiles(log_keep))
    keep_sum = _row_bcast(jnp.sum(keep_sum, axis=-1, keepdims=True))
    if first:
        acc_ref[h] = pv
        run_ref[h] = keep_sum
    else:
        acc_ref[h] = acc_ref[h] + pv
        run_ref[h] = run + keep_sum


def _stick_alive(state, heads):
    top = functools.reduce(jnp.maximum, [state[0][h] for h in heads])
    return (jnp.max(top) > STICK_FLOOR).astype(jnp.int32)


def _stick_head_out(state, h, rows):
    acc = state[1][h]
    lane = lax.broadcasted_iota(jnp.int32, (rows, LANES), 1)
    return jnp.where(lane < LANES // 2, acc[:rows], acc[rows:]).astype(BF16)


def _attn_state(kind, n_slabs, rows):
    n = 3 if kind == "diff" else 2
    return [pltpu.VMEM((n_slabs, rows, LANES), F32)] * n


def _rel_index(tq, tk):
    qi = lax.broadcasted_iota(jnp.int32, (2 * tq, tk), 0)
    qi = jnp.where(qi >= tq, qi - tq, qi)
    ki = lax.broadcasted_iota(jnp.int32, (2 * tq, tk), 1)
    return qi, ki


def _attn_prompt_kernel(*refs, kind, tq, d, lam_init):
    if kind == "diff":
        x_ref, mod_ref, gpost_ref, q_ref, k_ref, v_ref, wout_ref, lam_ref, sg_ref, y_ref, obuf, *state = refs
        lam = _diff_lambda(lam_ref, lam_init)
        sg = sg_ref[...]
    else:
        x_ref, mod_ref, gpost_ref, q_ref, k_ref, v_ref, wout_ref, y_ref, obuf, *state = refs
        ones_u = _suffix_ones(tq)
    j = pl.program_id(1)
    qi, ki = _rel_index(tq, tq)
    if kind == "diff":
        diag_mask = (ki // CHUNK) <= (qi // CHUNK)
    else:
        diag_mask = ki < qi

    def kv(kb, sl):
        start = pl.multiple_of(kb * tq, tq)
        return k_ref[0, pl.ds(start, tq), sl], v_ref[0, pl.ds(start, tq), sl]

    n_slabs = d // LANES
    for h0 in range(0, n_slabs, HEADS_PER_LOOP):
        heads = range(h0, min(h0 + HEADS_PER_LOOP, n_slabs))
        slabs = {h: slice(h * LANES, (h + 1) * LANES) for h in heads}
        q2s = {h: _stack_halves(q_ref[0, :, slabs[h]]) for h in heads}
        if kind == "diff":
            for h in heads:
                _softmax_block(q2s[h], *kv(j, slabs[h]), state, h, diag_mask, first=True)

            @pl.loop(0, j)
            def _(kb):
                for h in heads:
                    _softmax_block(q2s[h], *kv(kb, slabs[h]), state, h)

            for h in heads:
                obuf[:, slabs[h]] = _diff_head_out(state, h, tq, lam, sg, lam_init)
        else:
            for h in heads:
                _stick_block(q2s[h], *kv(j, slabs[h]), state, h, ones_u, diag_mask, first=True)

            def older_block(c):
                i, _ = c
                for h in heads:
                    _stick_block(q2s[h], *kv(j - 1 - i, slabs[h]), state, h, ones_u)
                return i + 1, _stick_alive(state, heads)

            lax.while_loop(lambda c: jnp.logical_and(c[0] < j, c[1] > 0), older_block,
                           (jnp.int32(0), _stick_alive(state, heads)))

            for h in heads:
                obuf[:, slabs[h]] = _stick_head_out(state, h, tq)

    m = _dot(obuf[...], wout_ref[...])
    y_ref[0] = x_ref[0] + mod_ref[0][2:3] * _rms(m, gpost_ref[...])


def _attn_prompt(kind, x, mod, gpost, qb, kb, vb, w_out, extra=(), lam_init=0.0):
    b, s, d = x.shape
    tq = min(ATTN_TILE, s)
    kern = functools.partial(_attn_prompt_kernel, kind=kind, tq=tq, d=d, lam_init=lam_init)
    tile = pl.BlockSpec((1, tq, d), lambda i, j: (i, j, 0))
    whole = pl.BlockSpec((1, s, d), lambda i, j: (i, 0, 0))
    return pl.pallas_call(
        kern,
        grid=(b, s // tq),
        in_specs=[tile, pl.BlockSpec((1, 6, d), lambda i, j: (i, 0, 0)), _const_spec((1, d)),
                  tile, whole, whole, _const_spec((d, d))] + [_const_spec(e.shape) for e in extra],
        out_specs=tile,
        out_shape=jax.ShapeDtypeStruct((b, s, d), F32),
        scratch_shapes=[pltpu.VMEM((tq, d), BF16)] + _attn_state(kind, d // LANES, 2 * tq),
        compiler_params=_params(2),
    )(x, mod, gpost, qb, kb, vb, w_out, *extra)


def _attn_stream_kernel(*refs, kind, s_new, past, tkc, d, lam_init, v_rows_by_head):
    if kind == "diff":
        (x_ref, mod_ref, gpost_ref, q_ref, kc_ref, vc_ref, kn_ref, vn_ref, wout_ref, lam_ref, sg_ref,
         y_ref, obuf, *state) = refs
        lam = _diff_lambda(lam_ref, lam_init)
        sg = sg_ref[...]
    else:
        x_ref, mod_ref, gpost_ref, q_ref, kc_ref, vc_ref, kn_ref, vn_ref, wout_ref, y_ref, obuf, *state = refs
        ones_new = _suffix_ones(kn_ref.shape[1])
        ones_cache = _suffix_ones(tkc)
    qi, ki = _rel_index(s_new, kn_ref.shape[1])
    if kind == "diff":
        new_mask = (((past + ki) // CHUNK) <= ((past + qi) // CHUNK)) & (ki < s_new)
    else:
        new_mask = (ki < qi) & (ki < s_new)
    n_cache = past // tkc

    heads = range(d // LANES)
    slabs = {h: slice(h * LANES, (h + 1) * LANES) for h in heads}
    q2s = {h: _stack_halves(q_ref[0, :, slabs[h]]) for h in heads}

    def cache_kv(kb, sl):
        rows = slice(kb * tkc, (kb + 1) * tkc)
        if v_rows_by_head:
            n_heads = d // LANES
            vs = vc_ref[0, pl.ds(kb * tkc * n_heads + sl.start // LANES, tkc, stride=n_heads), :]
        else:
            vs = vc_ref[0, rows, sl]
        return kc_ref[0, rows, sl].astype(BF16), vs.astype(BF16)

    if kind == "diff":
        for h in heads:
            _softmax_block(q2s[h], kn_ref[0, :, slabs[h]], vn_ref[0, :, slabs[h]], state, h, new_mask, first=True)
        for kb in range(n_cache):
            for h in heads:
                _softmax_block(q2s[h], *cache_kv(kb, slabs[h]), state, h)
        for h in heads:
            obuf[:, slabs[h]] = _diff_head_out(state, h, s_new, lam, sg, lam_init)
    else:
        for h in heads:
            _stick_block(q2s[h], kn_ref[0, :, slabs[h]], vn_ref[0, :, slabs[h]], state, h, ones_new, new_mask,
                         first=True)
        for kb in reversed(range(n_cache)):
            @pl.when(_stick_alive(state, heads) > 0)
            def _():
                for h in heads:
                    _stick_block(q2s[h], *cache_kv(kb, slabs[h]), state, h, ones_cache)
        for h in heads:
            obuf[:, slabs[h]] = _stick_head_out(state, h, s_new)

    m = _dot(obuf[...], wout_ref[...])
    y_ref[0] = x_ref[0] + mod_ref[0][2:3] * _rms(m, gpost_ref[...])


def _attn_stream(kind, x, mod, gpost, qb, k_cache, v_cache, kb_new, vb_new, w_out, extra=(), lam_init=0.0):
    b, s, d = x.shape
    v_rows_by_head = v_cache.shape[2] == LANES
    past = k_cache.shape[1]
    tkc = min(CACHE_TILE[kind], past)
    pad = (-s) % LANES
    kn = jnp.pad(kb_new, ((0, 0), (0, pad), (0, 0)))
    vn = jnp.pad(vb_new, ((0, 0), (0, pad), (0, 0)))
    kern = functools.partial(_attn_stream_kernel, kind=kind, s_new=s, past=past, tkc=tkc, d=d, lam_init=lam_init,
                             v_rows_by_head=v_rows_by_head)
    tile = pl.BlockSpec((1, s, d), lambda i: (i, 0, 0))
    cache = pl.BlockSpec((1, past, d), lambda i: (i, 0, 0))
    v_spec = pl.BlockSpec((1,) + v_cache.shape[1:], lambda i: (i, 0, 0))
    new = pl.BlockSpec((1, s + pad, d), lambda i: (i, 0, 0))
    return pl.pallas_call(
        kern,
        grid=(b,),
        in_specs=[tile, pl.BlockSpec((1, 6, d), lambda i: (i, 0, 0)), _const_spec((1, d)),
                  tile, cache, v_spec, new, new, _const_spec((d, d))] + [_const_spec(e.shape) for e in extra],
        out_specs=tile,
        out_shape=jax.ShapeDtypeStruct((b, s, d), F32),
        scratch_shapes=[pltpu.VMEM((s, d), BF16)] + _attn_state(kind, d // LANES, 2 * s),
        compiler_params=_params(1),
    )(x, mod, gpost, qb, k_cache, v_cache, kn, vn, w_out, *extra)


def _trunk(x, mod, pos, caches, wts):
    (norm_g, gm_w_in, gm_ln_g, gm_ln_b, gm_ws, gm_bs, gm_w_out,
     diff_w_qkv, diff_lambda, diff_subln_g, diff_w_out,
     sc_w_in, sc_conv_w, sc_w_out, sb_w_qkv, sb_w_out,
     ffn_w_in, ffn_conv_w, ffn_conv_b, ffn_w_out) = wts
    b, s, d = x.shape
    depth = norm_g.shape[0]
    dff = ffn_w_out.shape[1]
    gm_v, dk, dv, scs, sbk, sbv, ffs = [], [], [], [], [], [], []
    for i in range(depth):
        kind, j = i % N_MIXERS, i // N_MIXERS
        m_i = mod[i]
        g = [norm_g[i, n].reshape(1, d) for n in range(4)]
        if kind == 0:
            x, v_rows = _gmlp(x, m_i, g[0], g[1], gm_w_in[j], gm_ln_g[j], gm_ln_b[j], gm_ws[j], gm_bs[j],
                              gm_w_out[j])
            gm_v.append(v_rows)
        elif kind == 1:
            hd = d // (2 * DIFF_HEADS)
            lam_init = 0.8 - 0.6 * math.exp(-0.3 * i)
            k, v, qb, kb, vb = _qkv(x, m_i, g[0], diff_w_qkv[j], hd, rope=(pos, hd // 4))
            extra = (diff_lambda[j], diff_subln_g[j].reshape(1, 2 * hd))
            if caches is None:
                x = _attn_prompt("diff", x, m_i, g[1], qb, kb, vb, diff_w_out[j], extra, lam_init)
            else:
                kc = caches["diff_k"][j].reshape(b, -1, d)
                vc = caches["diff_v"][j].reshape(b, -1, 2 * hd)
                x = _attn_stream("diff", x, m_i, g[1], qb, kc, vc, kb, vb, diff_w_out[j], extra, lam_init)
            dk.append(k.reshape(b, s, DIFF_HEADS, 2, hd))
            dv.append(v.reshape(b, s, DIFF_HEADS, 2 * hd))
        elif kind == 2:
            prev = jnp.zeros((b, 2, d), F32) if caches is None else caches["sconv"][j]
            x, st = _sconv(x, m_i, g[0], g[1], prev, sc_w_in[j], sc_conv_w[j], sc_w_out[j])
            scs.append(st)
        else:
            hd = d // SB_HEADS
            k, v, qb, kb, vb = _qkv(x, m_i, g[0], sb_w_qkv[j], hd)
            if caches is None:
                x = _attn_prompt("stick", x, m_i, g[1], qb, kb, vb, sb_w_out[j])
            else:
                kc = caches["sb_k"][j].reshape(b, -1, d)
                vc = caches["sb_v"][j].reshape(b, -1, d)
                x = _attn_stream("stick", x, m_i, g[1], qb, kc, vc, kb, vb, sb_w_out[j])
            sbk.append(k.reshape(b, s, SB_HEADS, hd))
            sbv.append(v.reshape(b, s, SB_HEADS, hd))
        prev = jnp.zeros((b, 2, dff), F32) if caches is None else caches["ffn"][i]
        x, st = _ffn(x, m_i, g[2], g[3], prev, ffn_w_in[i], ffn_conv_w[i], ffn_conv_b[i], ffn_w_out[i])
        ffs.append(st)
    return (x, jnp.stack(gm_v), jnp.stack(dk), jnp.stack(dv), jnp.stack(scs),
            jnp.stack(sbk), jnp.stack(sbv), jnp.stack(ffs))


def kernel(x_prompt, x_sample, cache_diff_k, cache_diff_v, state_sconv, cache_sb_k, cache_sb_v, state_ffn_conv, c_prompt, c_sample, ada_w, ada_b, norm_g, gm_w_in, gm_ln_g, gm_ln_b, gm_ws, gm_bs, gm_w_out, diff_w_qkv, diff_lambda, diff_subln_g, diff_w_out, sc_w_in, sc_conv_w, sc_w_out, sb_w_qkv, sb_w_out, ffn_w_in, ffn_conv_w, ffn_conv_b, ffn_w_out):
    bp = x_prompt.shape[0]
    past = cache_diff_k.shape[2]
    cast = lambda w: w.astype(BF16)
    wts = (norm_g, cast(gm_w_in), gm_ln_g, gm_ln_b, gm_ws, gm_bs, cast(gm_w_out),
           cast(diff_w_qkv), diff_lambda, diff_subln_g, cast(diff_w_out),
           cast(sc_w_in), sc_conv_w, cast(sc_w_out), cast(sb_w_qkv), cast(sb_w_out),
           cast(ffn_w_in), ffn_conv_w, ffn_conv_b, cast(ffn_w_out))

    mod = _modulation(jnp.concatenate([c_prompt, c_sample], axis=0), ada_w, ada_b)
    mod = jnp.transpose(mod, (0, 2, 1, 3))
    mod_p, mod_s = mod[:, :bp], mod[:, bp:]

    pos_p = jnp.arange(x_prompt.shape[1])
    pos_s = past + jnp.arange(x_sample.shape[1])
    caches = {"diff_k": cache_diff_k, "diff_v": cache_diff_v, "sconv": state_sconv,
              "sb_k": cache_sb_k, "sb_v": cache_sb_v, "ffn": state_ffn_conv}

    (y_p, _, dk_p, dv_p, sc_p, sbk_p, sbv_p, ff_p) = _trunk(x_prompt, mod_p, pos_p, None, wts)
    (y_s, gmv_s, dk_s, dv_s, sc_s, sbk_s, sbv_s, ff_s) = _trunk(x_sample, mod_s, pos_s, caches, wts)
    return (y_p, y_s, gmv_s, dk_p, dv_p, dk_s, dv_s, sc_p, sc_s,
            sbk_p, sbv_p, sbk_s, sbv_s, ff_p, ff_s)
```

```python
import functools
import math

import jax
import jax.numpy as jnp
from jax import lax
from jax.experimental import pallas as pl
from jax.experimental.pallas import tpu as pltpu

F32 = jnp.float32
BF16 = jnp.bfloat16

EPS = 1e-6
CHUNK = 64
N_MIXERS = 4
GM_CHUNK = 128
GM_GROUPS = 8
DIFF_HEADS = 8
ROPE_THETA = 500000.0
SB_HEADS = 16

LANES = 128
CARRY_ROWS = 8
NEG_BIG = -1e30
LOG2E = math.log2(math.e)
STICK_FLOOR = -130.0

ROW_TILE = 512
ATTN_TILE = 256
CACHE_TILE = {"diff": 2048, "stick": 512}
CONV_COLS = 256
HEADS_PER_LOOP = 8
VMEM_LIMIT = 56 * 1024 * 1024


def _params(n_axes):
    return pltpu.CompilerParams(dimension_semantics=("arbitrary",) * n_axes,
                                vmem_limit_bytes=VMEM_LIMIT)


def _const_spec(shape):
    zeros = (0,) * len(shape)
    return pl.BlockSpec(shape, lambda *_: zeros, pipeline_mode=pl.Buffered(1))


def _col_block_specs(rows, width, n):
    return [pl.BlockSpec((rows, width // n), lambda *_, k=k: (0, k), pipeline_mode=pl.Buffered(1))
            for k in range(n)]


def _rms(x, g):
    return x * lax.rsqrt(jnp.mean(x * x, axis=-1, keepdims=True) + EPS) * g


def _prenorm(x, g, scale, shift):
    return _rms(x, g) * (1.0 + scale) + shift


def _dot(a, b):
    return jnp.dot(a, b, preferred_element_type=F32)


def _dot_nt(a, b):
    return lax.dot_general(a, b, (((1,), (1,)), ((), ())), preferred_element_type=F32)


def _mod_kernel(c_ref, w_ref, b_ref, o_ref):
    c = c_ref[...]
    cs = (c * jax.nn.sigmoid(c)).astype(BF16)
    o_ref[0] = _dot(cs, w_ref[0].astype(BF16)) + b_ref[0]


def _modulation(c_all, ada_w, ada_b):
    depth, d, _ = ada_w.shape
    rows = c_all.shape[0]
    out = pl.pallas_call(
        _mod_kernel,
        grid=(depth, 6),
        in_specs=[
            pl.BlockSpec((rows, d), lambda i, j: (0, 0)),
            pl.BlockSpec((1, d, d), lambda i, j: (i, 0, j)),
            pl.BlockSpec((1, 1, d), lambda i, j: (i * 6 + j, 0, 0)),
        ],
        out_specs=pl.BlockSpec((1, rows, d), lambda i, j: (i * 6 + j, 0, 0)),
        out_shape=jax.ShapeDtypeStruct((depth * 6, rows, d), F32),
        compiler_params=_params(2),
    )(c_all, ada_w, ada_b.reshape(depth * 6, 1, d))
    return out.reshape(depth, 6, rows, d)


def _row_tiling(b, s):
    tm = min(ROW_TILE, s)
    bt = max(1, min(b, ROW_TILE // tm))
    while b % bt:
        bt -= 1
    return bt, tm


def _tile_spec(bt, tm, width):
    return pl.BlockSpec((bt, tm, width), lambda i, j: (i, j, 0))


def _batch_spec(bt, rows, width):
    return pl.BlockSpec((bt, rows, width), lambda i, j: (i, 0, 0))


def _prenorm_rows(x3, g, mod3, scale_row, shift_row):
    bt, tm, d = x3.shape
    h3 = _prenorm(x3, g, mod3[:, scale_row:scale_row + 1], mod3[:, shift_row:shift_row + 1])
    return h3.reshape(bt * tm, d).astype(BF16)


def _gated_residual(x3, mod3, gate_row, m, gpost):
    return x3 + mod3[:, gate_row:gate_row + 1] * _rms(m, gpost).reshape(x3.shape)


def _conv3(val, cw, hist, lo, width):
    bt, rows, _ = hist.shape
    tm = rows - CARRY_ROWS
    cols = slice(lo, lo + width)
    hist[:, CARRY_ROWS:, cols] = val.reshape(bt, tm, width)
    back1 = hist[:, CARRY_ROWS - 1:CARRY_ROWS - 1 + tm, cols].reshape(bt * tm, width)
    back2 = hist[:, CARRY_ROWS - 2:CARRY_ROWS - 2 + tm, cols].reshape(bt * tm, width)
    hist[:, 0:CARRY_ROWS, cols] = hist[:, tm:tm + CARRY_ROWS, cols]
    return cw[2:3] * val + cw[1:2] * back1 + cw[0:1] * back2


def _init_hist(hist, prev_ref):
    @pl.when(pl.program_id(1) == 0)
    def _():
        hist[:, 0:CARRY_ROWS, :] = jnp.zeros((hist.shape[0], CARRY_ROWS, hist.shape[2]), F32)
        hist[:, CARRY_ROWS - 2:CARRY_ROWS, :] = prev_ref[...]


def _ffn_kernel(x_ref, mod_ref, gpre_ref, gpost_ref, prev_ref, wg_ref, wu_ref, cw_ref, cb_ref, wout_ref,
                y_ref, st_ref, hist, act, *, dff, fc):
    _init_hist(hist, prev_ref)
    x3 = x_ref[...]
    mod3 = mod_ref[...]
    h = _prenorm_rows(x3, gpre_ref[...], mod3, 4, 3)
    for c in range(dff // fc):
        lo = c * fc
        g = _dot(h, wg_ref[:, lo:lo + fc])
        u = _dot(h, wu_ref[:, lo:lo + fc])
        y = _conv3(g, cw_ref[:, lo:lo + fc], hist, lo, fc) + cb_ref[:, lo:lo + fc]
        act[:, lo:lo + fc] = (y * jax.nn.sigmoid(y) * u).astype(BF16)
    st_ref[...] = hist[:, CARRY_ROWS - 2:CARRY_ROWS, :]
    y_ref[...] = _gated_residual(x3, mod3, 5, _dot(act[...], wout_ref[...]), gpost_ref[...])


def _ffn(x, mod, gpre, gpost, prev, w_in, conv_w, conv_b, w_out):
    b, s, d = x.shape
    dff = w_out.shape[0]
    bt, tm = _row_tiling(b, s)
    kern = functools.partial(_ffn_kernel, dff=dff, fc=CONV_COLS)
    return pl.pallas_call(
        kern,
        grid=(b // bt, s // tm),
        in_specs=[
            _tile_spec(bt, tm, d),
            _batch_spec(bt, 6, d),
            _const_spec((1, d)),
            _const_spec((1, d)),
            _batch_spec(bt, 2, dff),
            *_col_block_specs(d, 2 * dff, 2),
            _const_spec((3, dff)),
            _const_spec((1, dff)),
            _const_spec((dff, d)),
        ],
        out_specs=[_tile_spec(bt, tm, d), _batch_spec(bt, 2, dff)],
        out_shape=[
            jax.ShapeDtypeStruct((b, s, d), F32),
            jax.ShapeDtypeStruct((b, 2, dff), F32),
        ],
        scratch_shapes=[pltpu.VMEM((bt, CARRY_ROWS + tm, dff), F32), pltpu.VMEM((bt * tm, dff), BF16)],
        compiler_params=_params(2),
    )(x, mod, gpre, gpost, prev, w_in, w_in, conv_w, conv_b.reshape(1, dff), w_out)


def _sconv_kernel(x_ref, mod_ref, gpre_ref, gpost_ref, prev_ref, wb_ref, wc_ref, wx_ref, cw_ref, wout_ref,
                  y_ref, st_ref, hist, act, *, dsc, fc):
    _init_hist(hist, prev_ref)
    x3 = x_ref[...]
    mod3 = mod_ref[...]
    h = _prenorm_rows(x3, gpre_ref[...], mod3, 1, 0)
    for c in range(dsc // fc):
        lo = c * fc
        b_gate = _dot(h, wb_ref[:, lo:lo + fc])
        c_gate = _dot(h, wc_ref[:, lo:lo + fc])
        xin = _dot(h, wx_ref[:, lo:lo + fc])
        y = _conv3(c_gate * xin, cw_ref[:, lo:lo + fc], hist, lo, fc)
        act[:, lo:lo + fc] = (b_gate * y).astype(BF16)
    st_ref[...] = hist[:, CARRY_ROWS - 2:CARRY_ROWS, :]
    y_ref[...] = _gated_residual(x3, mod3, 2, _dot(act[...], wout_ref[...]), gpost_ref[...])


def _sconv(x, mod, gpre, gpost, prev, w_in, conv_w, w_out):
    b, s, d = x.shape
    dsc = w_out.shape[0]
    bt, tm = _row_tiling(b, s)
    kern = functools.partial(_sconv_kernel, dsc=dsc, fc=CONV_COLS)
    return pl.pallas_call(
        kern,
        grid=(b // bt, s // tm),
        in_specs=[
            _tile_spec(bt, tm, d),
            _batch_spec(bt, 6, d),
            _const_spec((1, d)),
            _const_spec((1, d)),
            _batch_spec(bt, 2, dsc),
            *_col_block_specs(d, 3 * dsc, 3),
            _const_spec((3, dsc)),
            _const_spec((dsc, d)),
        ],
        out_specs=[_tile_spec(bt, tm, d), _batch_spec(bt, 2, dsc)],
        out_shape=[
            jax.ShapeDtypeStruct((b, s, d), F32),
            jax.ShapeDtypeStruct((b, 2, dsc), F32),
        ],
        scratch_shapes=[pltpu.VMEM((bt, CARRY_ROWS + tm, dsc), F32), pltpu.VMEM((bt * tm, dsc), BF16)],
        compiler_params=_params(2),
    )(x, mod, gpre, gpost, prev, w_in, w_in, w_in, conv_w, w_out)


def _gmlp_kernel(x_ref, mod_ref, gpre_ref, gpost_ref, wu_ref, wv_ref, lng_ref, lnb_ref, ws_ref, bsf_ref,
                 wout_ref, y_ref, v_ref, mixbuf, *, span, dgm):
    x3 = x_ref[...]
    mod3 = mod_ref[...]
    h = _prenorm_rows(x3, gpre_ref[...], mod3, 1, 0)
    rows = h.shape[0]
    u = jax.nn.gelu(_dot(h, wu_ref[...]))
    v = jax.nn.gelu(_dot(h, wv_ref[...]))
    vc = v - jnp.mean(v, axis=-1, keepdims=True)
    v = vc * lax.rsqrt(jnp.mean(vc * vc, axis=-1, keepdims=True) + EPS) * lng_ref[...] + lnb_ref[...]
    v_ref[...] = v.reshape(v_ref.shape)
    mix_dtype = BF16 if span % 16 == 0 and span >= LANES else F32
    vm = v.astype(mix_dtype)
    gw = dgm // GM_GROUPS
    row = lax.broadcasted_iota(jnp.int32, (span, span), 0)
    col = lax.broadcasted_iota(jnp.int32, (span, span), 1)
    for g in range(GM_GROUPS):
        w = jnp.where(row >= col, ws_ref[g], 0.0).astype(mix_dtype)
        for r in range(rows // span):
            mixbuf[r * span:(r + 1) * span, g * gw:(g + 1) * gw] = (
                _dot(w, vm[r * span:(r + 1) * span, g * gw:(g + 1) * gw]) + bsf_ref[:, g * gw:(g + 1) * gw])
    m = _dot((u * mixbuf[...]).astype(BF16), wout_ref[...])
    y_ref[...] = _gated_residual(x3, mod3, 2, m, gpost_ref[...])


def _gmlp(x, mod, gpre, gpost, w_in, ln_g, ln_b, ws, bs, w_out):
    b, s, d = x.shape
    dgm = w_out.shape[0]
    span = GM_CHUNK if s >= GM_CHUNK else s
    bt, tm = _row_tiling(b, s)
    ws_l = ws[:, :span, :span]
    bs_tile = jnp.repeat(bs[:, :span].T, dgm // GM_GROUPS, axis=1)
    kern = functools.partial(_gmlp_kernel, span=span, dgm=dgm)
    return pl.pallas_call(
        kern,
        grid=(b // bt, s // tm),
        in_specs=[
            _tile_spec(bt, tm, d),
            _batch_spec(bt, 6, d),
            _const_spec((1, d)),
            _const_spec((1, d)),
            *_col_block_specs(d, 2 * dgm, 2),
            _const_spec((1, dgm)),
            _const_spec((1, dgm)),
            _const_spec((GM_GROUPS, span, span)),
            _const_spec((span, dgm)),
            _const_spec((dgm, d)),
        ],
        out_specs=[_tile_spec(bt, tm, d), _tile_spec(bt, tm, dgm)],
        out_shape=[
            jax.ShapeDtypeStruct((b, s, d), F32),
            jax.ShapeDtypeStruct((b, s, dgm), F32),
        ],
        scratch_shapes=[pltpu.VMEM((bt * tm, dgm), F32)],
        compiler_params=_params(2),
    )(x, mod, gpre, gpost, w_in, w_in, ln_g.reshape(1, dgm), ln_b.reshape(1, dgm), ws_l, bs_tile, w_out)


def _rotate(t, cos, sin_fwd, sin_bwd, rot_half):
    outs = []
    for s in range(t.shape[1] // LANES):
        ts = t[:, s * LANES:(s + 1) * LANES]
        ahead = pltpu.roll(ts, LANES - rot_half, 1)
        behind = pltpu.roll(ts, rot_half, 1)
        outs.append(ts * cos + ahead * sin_fwd + behind * sin_bwd)
    return jnp.concatenate(outs, axis=1)


def _qkv_kernel(*refs, qscale, rot_half):
    if rot_half:
        (x_ref, mod_ref, gpre_ref, wq_ref, wk_ref, wv_ref, cos_ref, sf_ref, sb_ref,
         k_ref, v_ref, qb_ref, kb_ref, vb_ref) = refs
    else:
        x_ref, mod_ref, gpre_ref, wq_ref, wk_ref, wv_ref, k_ref, v_ref, qb_ref, kb_ref, vb_ref = refs
    x3 = x_ref[...]
    bt = x3.shape[0]
    h = _prenorm_rows(x3, gpre_ref[...], mod_ref[...], 1, 0)
    q = _dot(h, wq_ref[...])
    k = _dot(h, wk_ref[...])
    v = _dot(h, wv_ref[...])
    if rot_half:
        cos, sf, sb = (jnp.concatenate([t[...]] * bt, axis=0) for t in (cos_ref, sf_ref, sb_ref))
        q = _rotate(q, cos, sf, sb, rot_half)
        k = _rotate(k, cos, sf, sb, rot_half)
    k_ref[...] = k.reshape(x3.shape)
    v_ref[...] = v.reshape(x3.shape)
    qb_ref[...] = (q * qscale).astype(BF16).reshape(x3.shape)
    kb_ref[...] = k.astype(BF16).reshape(x3.shape)
    vb_ref[...] = v.astype(BF16).reshape(x3.shape)


def _rope_tables(pos, head_dim, rot_dim):
    half = rot_dim // 2
    inv = ROPE_THETA ** (-jnp.arange(0, rot_dim, 2, dtype=F32) / rot_dim)
    ang = pos.astype(F32)[:, None] * inv[None, :]
    cos, sin = jnp.cos(ang), jnp.sin(ang)
    n = pos.shape[0]
    pad = jnp.zeros((n, head_dim - rot_dim), F32)
    zeros = jnp.zeros((n, half), F32)
    cos_t = jnp.concatenate([cos, cos, pad + 1.0], axis=1)
    sf_t = jnp.concatenate([-sin, zeros, pad], axis=1)
    sb_t = jnp.concatenate([zeros, sin, pad], axis=1)
    rep = LANES // head_dim
    return tuple(jnp.tile(t, (1, rep)) for t in (cos_t, sf_t, sb_t))


def _qkv(x, mod, gpre, w_qkv, head_dim, rope=None):
    b, s, d = x.shape
    bt, tm = _row_tiling(b, s)
    rot_half = 0
    tables = ()
    table_specs = []
    if rope is not None:
        pos, rot_dim = rope
        rot_half = rot_dim // 2
        tables = _rope_tables(pos, head_dim, rot_dim)
        table_specs = [pl.BlockSpec((tm, LANES), lambda i, j: (j, 0))] * 3
    kern = functools.partial(_qkv_kernel, qscale=head_dim ** -0.5 * LOG2E, rot_half=rot_half)
    tile = _tile_spec(bt, tm, d)
    return pl.pallas_call(
        kern,
        grid=(b // bt, s // tm),
        in_specs=[tile, _batch_spec(bt, 6, d), _const_spec((1, d)), *_col_block_specs(d, 3 * d, 3)] + table_specs,
        out_specs=[tile] * 5,
        out_shape=[jax.ShapeDtypeStruct((b, s, d), F32)] * 2 + [jax.ShapeDtypeStruct((b, s, d), BF16)] * 3,
        compiler_params=_params(2),
    )(x, mod, gpre, w_qkv, w_qkv, w_qkv, *tables)


def _stack_halves(qh):
    lane = lax.broadcasted_iota(jnp.int32, qh.shape, 1)
    zero = jnp.zeros_like(qh)
    return jnp.concatenate([jnp.where(lane < LANES // 2, qh, zero),
                            jnp.where(lane >= LANES // 2, qh, zero)], axis=0)


def _lane_tiles(x):
    return [x[:, t * LANES:(t + 1) * LANES] for t in range(x.shape[1] // LANES)]


def _row_bcast(col):
    return jnp.broadcast_to(col, (col.shape[0], LANES))


def _softmax_block(q2, ks, vs, state, h, mask=None, first=False):
    m_ref, l_ref, acc_ref = state
    s = _dot_nt(q2, ks)
    if mask is not None:
        s = jnp.where(mask, s, NEG_BIG)
    tiles = _lane_tiles(s)
    m_blk = _row_bcast(jnp.max(functools.reduce(jnp.maximum, tiles), axis=-1, keepdims=True))
    m_new = m_blk if first else jnp.maximum(m_ref[h], m_blk)
    ps = [jnp.exp2(t - m_new) for t in tiles]
    p_sum = functools.reduce(jnp.add, ps)
    pv = _dot(jnp.concatenate(ps, axis=1).astype(BF16), vs)
    if first:
        l_ref[h] = p_sum
        acc_ref[h] = pv
    else:
        alpha = jnp.exp2(m_ref[h] - m_new)
        l_ref[h] = alpha * l_ref[h] + p_sum
        acc_ref[h] = alpha * acc_ref[h] + pv
    m_ref[h] = m_new


def _diff_lambda(lam_ref, lam_init):
    lp = lam_ref[...]
    s1 = jnp.sum(lp[0:1] * lp[1:2], axis=-1, keepdims=True)
    s2 = jnp.sum(lp[2:3] * lp[3:4], axis=-1, keepdims=True)
    return jnp.exp(s1) - jnp.exp(s2) + lam_init


def _diff_head_out(state, h, rows, lam, sg, lam_init):
    _, l_ref, acc_ref = state
    l, acc = jnp.sum(l_ref[h], axis=-1, keepdims=True), acc_ref[h]
    o = acc[:rows] / l[:rows] - lam * (acc[rows:] / l[rows:])
    return (_rms(o, sg) * (1.0 - lam_init)).astype(BF16)


def _suffix_ones(n):
    row = lax.broadcasted_iota(jnp.int32, (n, n), 0)
    col = lax.broadcasted_iota(jnp.int32, (n, n), 1)
    return jnp.where(row > col, 1.0, 0.0).astype(BF16)


def _stick_block(q2, ks, vs, state, h, ones_u, mask=None, first=False):
    run_ref, acc_ref = state
    z = _dot_nt(q2, ks)
    log_beta = jnp.minimum(z, 0.0) - jnp.log2(1.0 + jnp.exp2(-jnp.abs(z)))
    log_keep = log_beta - z
    if mask is not None:
        log_keep = jnp.where(mask, log_keep, 0.0)
    expo = log_beta + _dot(log_keep.astype(BF16), ones_u)
    if not first:
        run = run_ref[h]
        expo = expo + jnp.concatenate([run] * (z.shape[1] // LANES), axis=1)
    w = jnp.exp2(expo)
    if mask is not None:
        w = jnp.where(mask, w, 0.0)
    pv = _dot(w.astype(BF16), vs)
    keep_sum = functools.reduce(jnp.add, _lane_tiles(log_keep))
    keep_sum = _row_bcast(jnp.sum(keep_sum, axis=-1, keepdims=True))
    if first:
        acc_ref[h] = pv
        run_ref[h] = keep_sum
    else:
        acc_ref[h] = acc_ref[h] + pv
        run_ref[h] = run + keep_sum


def _stick_alive(state, heads):
    top = functools.reduce(jnp.maximum, [state[0][h] for h in heads])
    return (jnp.max(top) > STICK_FLOOR).astype(jnp.int32)


def _stick_head_out(state, h, rows):
    acc = state[1][h]
    lane = lax.broadcasted_iota(jnp.int32, (rows, LANES), 1)
    return jnp.where(lane < LANES // 2, acc[:rows], acc[rows:]).astype(BF16)


def _attn_state(kind, n_slabs, rows):
    n = 3 if kind == "diff" else 2
    return [pltpu.VMEM((n_slabs, rows, LANES), F32)] * n


def _rel_index(tq, tk):
    qi = lax.broadcasted_iota(jnp.int32, (2 * tq, tk), 0)
    qi = jnp.where(qi >= tq, qi - tq, qi)
    ki = lax.broadcasted_iota(jnp.int32, (2 * tq, tk), 1)
    return qi, ki


def _attn_prompt_kernel(*refs, kind, tq, d, lam_init):
    if kind == "diff":
        x_ref, mod_ref, gpost_ref, q_ref, k_ref, v_ref, wout_ref, lam_ref, sg_ref, y_ref, obuf, *state = refs
        lam = _diff_lambda(lam_ref, lam_init)
        sg = sg_ref[...]
    else:
        x_ref, mod_ref, gpost_ref, q_ref, k_ref, v_ref, wout_ref, y_ref, obuf, *state = refs
        ones_u = _suffix_ones(tq)
    j = pl.program_id(1)
    qi, ki = _rel_index(tq, tq)
    if kind == "diff":
        diag_mask = (ki // CHUNK) <= (qi // CHUNK)
    else:
        diag_mask = ki < qi

    def kv(kb, sl):
        start = pl.multiple_of(kb * tq, tq)
        return k_ref[0, pl.ds(start, tq), sl], v_ref[0, pl.ds(start, tq), sl]

    n_slabs = d // LANES
    for h0 in range(0, n_slabs, HEADS_PER_LOOP):
        heads = range(h0, min(h0 + HEADS_PER_LOOP, n_slabs))
        slabs = {h: slice(h * LANES, (h + 1) * LANES) for h in heads}
        q2s = {h: _stack_halves(q_ref[0, :, slabs[h]]) for h in heads}
        if kind == "diff":
            for h in heads:
                _softmax_block(q2s[h], *kv(j, slabs[h]), state, h, diag_mask, first=True)

            @pl.loop(0, j)
            def _(kb):
                for h in heads:
                    _softmax_block(q2s[h], *kv(kb, slabs[h]), state, h)

            for h in heads:
                obuf[:, slabs[h]] = _diff_head_out(state, h, tq, lam, sg, lam_init)
        else:
            def diag_blocks():
                for h in heads:
                    _stick_block(q2s[h], *kv(j, slabs[h]), state, h, ones_u, diag_mask, first=True)

            def older_blocks(i):
                for h in heads:
                    _stick_block(q2s[h], *kv(j - 1 - i, slabs[h]), state, h, ones_u)

            pl.when(j == 0)(diag_blocks)

            @pl.when(j > 0)
            def _():
                diag_blocks()
                older_blocks(0)

            def older_step(c):
                older_blocks(c[0])
                return c[0] + 1, _stick_alive(state, heads)

            lax.while_loop(lambda c: jnp.logical_and(c[0] < j, c[1] > 0), older_step,
                           (jnp.int32(1), _stick_alive(state, heads)))

            for h in heads:
                obuf[:, slabs[h]] = _stick_head_out(state, h, tq)

    m = _dot(obuf[...], wout_ref[...])
    y_ref[0] = x_ref[0] + mod_ref[0][2:3] * _rms(m, gpost_ref[...])


def _attn_prompt(kind, x, mod, gpost, qb, kb, vb, w_out, extra=(), lam_init=0.0):
    b, s, d = x.shape
    tq = min(ATTN_TILE, s)
    kern = functools.partial(_attn_prompt_kernel, kind=kind, tq=tq, d=d, lam_init=lam_init)
    tile = pl.BlockSpec((1, tq, d), lambda i, j: (i, j, 0))
    whole = pl.BlockSpec((1, s, d), lambda i, j: (i, 0, 0))
    return pl.pallas_call(
        kern,
        grid=(b, s // tq),
        in_specs=[tile, pl.BlockSpec((1, 6, d), lambda i, j: (i, 0, 0)), _const_spec((1, d)),
                  tile, whole, whole, _const_spec((d, d))] + [_const_spec(e.shape) for e in extra],
        out_specs=tile,
        out_shape=jax.ShapeDtypeStruct((b, s, d), F32),
        scratch_shapes=[pltpu.VMEM((tq, d), BF16)] + _attn_state(kind, d // LANES, 2 * tq),
        compiler_params=_params(2),
    )(x, mod, gpost, qb, kb, vb, w_out, *extra)


def _attn_stream_kernel(*refs, kind, s_new, past, tkc, d, lam_init, v_rows_by_head):
    if kind == "diff":
        (x_ref, mod_ref, gpost_ref, q_ref, kc_ref, vc_ref, kn_ref, vn_ref, wout_ref, lam_ref, sg_ref,
         y_ref, obuf, *state) = refs
        lam = _diff_lambda(lam_ref, lam_init)
        sg = sg_ref[...]
    else:
        x_ref, mod_ref, gpost_ref, q_ref, kc_ref, vc_ref, kn_ref, vn_ref, wout_ref, y_ref, obuf, *state = refs
        ones_new = _suffix_ones(kn_ref.shape[1])
        ones_cache = _suffix_ones(tkc)
    qi, ki = _rel_index(s_new, kn_ref.shape[1])
    if kind == "diff":
        new_mask = (((past + ki) // CHUNK) <= ((past + qi) // CHUNK)) & (ki < s_new)
    else:
        new_mask = (ki < qi) & (ki < s_new)
    n_cache = past // tkc

    heads = range(d // LANES)
    slabs = {h: slice(h * LANES, (h + 1) * LANES) for h in heads}
    q2s = {h: _stack_halves(q_ref[0, :, slabs[h]]) for h in heads}

    def cache_kv(kb, sl):
        rows = slice(kb * tkc, (kb + 1) * tkc)
        if v_rows_by_head:
            n_heads = d // LANES
            vs = vc_ref[0, pl.ds(kb * tkc * n_heads + sl.start // LANES, tkc, stride=n_heads), :]
        else:
            vs = vc_ref[0, rows, sl]
        return kc_ref[0, rows, sl].astype(BF16), vs.astype(BF16)

    if kind == "diff":
        for h in heads:
            _softmax_block(q2s[h], kn_ref[0, :, slabs[h]], vn_ref[0, :, slabs[h]], state, h, new_mask, first=True)
        for kb in range(n_cache):
            for h in heads:
                _softmax_block(q2s[h], *cache_kv(kb, slabs[h]), state, h)
        for h in heads:
            obuf[:, slabs[h]] = _diff_head_out(state, h, s_new, lam, sg, lam_init)
    else:
        for h in heads:
            _stick_block(q2s[h], kn_ref[0, :, slabs[h]], vn_ref[0, :, slabs[h]], state, h, ones_new, new_mask,
                         first=True)
        for kb in reversed(range(n_cache)):
            @pl.when(_stick_alive(state, heads) > 0)
            def _():
                for h in heads:
                    _stick_block(q2s[h], *cache_kv(kb, slabs[h]), state, h, ones_cache)
        for h in heads:
            obuf[:, slabs[h]] = _stick_head_out(state, h, s_new)

    m = _dot(obuf[...], wout_ref[...])
    y_ref[0] = x_ref[0] + mod_ref[0][2:3] * _rms(m, gpost_ref[...])


def _attn_stream(kind, x, mod, gpost, qb, k_cache, v_cache, kb_new, vb_new, w_out, extra=(), lam_init=0.0):
    b, s, d = x.shape
    v_rows_by_head = v_cache.shape[2] == LANES
    past = k_cache.shape[1]
    tkc = min(CACHE_TILE[kind], past)
    pad = (-s) % LANES
    kn = jnp.pad(kb_new, ((0, 0), (0, pad), (0, 0)))
    vn = jnp.pad(vb_new, ((0, 0), (0, pad), (0, 0)))
    kern = functools.partial(_attn_stream_kernel, kind=kind, s_new=s, past=past, tkc=tkc, d=d, lam_init=lam_init,
                             v_rows_by_head=v_rows_by_head)
    tile = pl.BlockSpec((1, s, d), lambda i: (i, 0, 0))
    cache = pl.BlockSpec((1, past, d), lambda i: (i, 0, 0))
    v_spec = pl.BlockSpec((1,) + v_cache.shape[1:], lambda i: (i, 0, 0))
    new = pl.BlockSpec((1, s + pad, d), lambda i: (i, 0, 0))
    return pl.pallas_call(
        kern,
        grid=(b,),
        in_specs=[tile, pl.BlockSpec((1, 6, d), lambda i: (i, 0, 0)), _const_spec((1, d)),
                  tile, cache, v_spec, new, new, _const_spec((d, d))] + [_const_spec(e.shape) for e in extra],
        out_specs=tile,
        out_shape=jax.ShapeDtypeStruct((b, s, d), F32),
        scratch_shapes=[pltpu.VMEM((s, d), BF16)] + _attn_state(kind, d // LANES, 2 * s),
        compiler_params=_params(1),
    )(x, mod, gpost, qb, k_cache, v_cache, kn, vn, w_out, *extra)


def _trunk(x, mod, pos, caches, wts):
    (norm_g, gm_w_in, gm_ln_g, gm_ln_b, gm_ws, gm_bs, gm_w_out,
     diff_w_qkv, diff_lambda, diff_subln_g, diff_w_out,
     sc_w_in, sc_conv_w, sc_w_out, sb_w_qkv, sb_w_out,
     ffn_w_in, ffn_conv_w, ffn_conv_b, ffn_w_out) = wts
    b, s, d = x.shape
    depth = norm_g.shape[0]
    dff = ffn_w_out.shape[1]
    gm_v, dk, dv, scs, sbk, sbv, ffs = [], [], [], [], [], [], []
    for i in range(depth):
        kind, j = i % N_MIXERS, i // N_MIXERS
        m_i = mod[i]
        g = [norm_g[i, n].reshape(1, d) for n in range(4)]
        if kind == 0:
            x, v_rows = _gmlp(x, m_i, g[0], g[1], gm_w_in[j], gm_ln_g[j], gm_ln_b[j], gm_ws[j], gm_bs[j],
                              gm_w_out[j])
            gm_v.append(v_rows)
        elif kind == 1:
            hd = d // (2 * DIFF_HEADS)
            lam_init = 0.8 - 0.6 * math.exp(-0.3 * i)
            k, v, qb, kb, vb = _qkv(x, m_i, g[0], diff_w_qkv[j], hd, rope=(pos, hd // 4))
            extra = (diff_lambda[j], diff_subln_g[j].reshape(1, 2 * hd))
            if caches is None:
                x = _attn_prompt("diff", x, m_i, g[1], qb, kb, vb, diff_w_out[j], extra, lam_init)
            else:
                kc = caches["diff_k"][j].reshape(b, -1, d)
                vc = caches["diff_v"][j].reshape(b, -1, 2 * hd)
                x = _attn_stream("diff", x, m_i, g[1], qb, kc, vc, kb, vb, diff_w_out[j], extra, lam_init)
            dk.append(k.reshape(b, s, DIFF_HEADS, 2, hd))
            dv.append(v.reshape(b, s, DIFF_HEADS, 2 * hd))
        elif kind == 2:
            prev = jnp.zeros((b, 2, d), F32) if caches is None else caches["sconv"][j]
            x, st = _sconv(x, m_i, g[0], g[1], prev, sc_w_in[j], sc_conv_w[j], sc_w_out[j])
            scs.append(st)
        else:
            hd = d // SB_HEADS
            k, v, qb, kb, vb = _qkv(x, m_i, g[0], sb_w_qkv[j], hd)
            if caches is None:
                x = _attn_prompt("stick", x, m_i, g[1], qb, kb, vb, sb_w_out[j])
            else:
                kc = caches["sb_k"][j].reshape(b, -1, d)
                vc = caches["sb_v"][j].reshape(b, -1, d)
                x = _attn_stream("stick", x, m_i, g[1], qb, kc, vc, kb, vb, sb_w_out[j])
            sbk.append(k.reshape(b, s, SB_HEADS, hd))
            sbv.append(v.reshape(b, s, SB_HEADS, hd))
        prev = jnp.zeros((b, 2, dff), F32) if caches is None else caches["ffn"][i]
        x, st = _ffn(x, m_i, g[2], g[3], prev, ffn_w_in[i], ffn_conv_w[i], ffn_conv_b[i], ffn_w_out[i])
        ffs.append(st)
    return (x, jnp.stack(gm_v), jnp.stack(dk), jnp.stack(dv), jnp.stack(scs),
            jnp.stack(sbk), jnp.stack(sbv), jnp.stack(ffs))


def kernel(x_prompt, x_sample, cache_diff_k, cache_diff_v, state_sconv, cache_sb_k, cache_sb_v, state_ffn_conv, c_prompt, c_sample, ada_w, ada_b, norm_g, gm_w_in, gm_ln_g, gm_ln_b, gm_ws, gm_bs, gm_w_out, diff_w_qkv, diff_lambda, diff_subln_g, diff_w_out, sc_w_in, sc_conv_w, sc_w_out, sb_w_qkv, sb_w_out, ffn_w_in, ffn_conv_w, ffn_conv_b, ffn_w_out):
    bp = x_prompt.shape[0]
    past = cache_diff_k.shape[2]
    cast = lambda w: w.astype(BF16)
    wts = (norm_g, cast(gm_w_in), gm_ln_g, gm_ln_b, gm_ws, gm_bs, cast(gm_w_out),
           cast(diff_w_qkv), diff_lambda, diff_subln_g, cast(diff_w_out),
           cast(sc_w_in), sc_conv_w, cast(sc_w_out), cast(sb_w_qkv), cast(sb_w_out),
           cast(ffn_w_in), ffn_conv_w, ffn_conv_b, cast(ffn_w_out))

    mod = _modulation(jnp.concatenate([c_prompt, c_sample], axis=0), ada_w, ada_b)
    mod = jnp.transpose(mod, (0, 2, 1, 3))
    mod_p, mod_s = mod[:, :bp], mod[:, bp:]

    pos_p = jnp.arange(x_prompt.shape[1])
    pos_s = past + jnp.arange(x_sample.shape[1])
    caches = {"diff_k": cache_diff_k, "diff_v": cache_diff_v, "sconv": state_sconv,
              "sb_k": cache_sb_k, "sb_v": cache_sb_v, "ffn": state_ffn_conv}

    (y_p, _, dk_p, dv_p, sc_p, sbk_p, sbv_p, ff_p) = _trunk(x_prompt, mod_p, pos_p, None, wts)
    (y_s, gmv_s, dk_s, dv_s, sc_s, sbk_s, sbv_s, ff_s) = _trunk(x_sample, mod_s, pos_s, caches, wts)
    return (y_p, y_s, gmv_s, dk_p, dv_p, dk_s, dv_s, sc_p, sc_s,
            sbk_p, sbv_p, sbk_s, sbv_s, ff_p, ff_s)
```
